```python
import math
import jax, jax.numpy as jnp
from jax import lax
import numpy as np

D_MODEL = 2048
BATCH = 16
SEQ = 2048
DEPTH = 1

HEAD_DIM = 128
MIX_WIDTH = D_MODEL
N_HEADS_DIFF = MIX_WIDTH // (2 * HEAD_DIM)
N_HEADS_DIL = MIX_WIDTH // (2 * HEAD_DIM)
DIFF_QK_DIM = HEAD_DIM // 2
DIFF_WIDTH = N_HEADS_DIFF * HEAD_DIM
DIL_WIDTH = N_HEADS_DIL * HEAD_DIM
IN_COLS = 3 * DIFF_WIDTH + 3 * DIL_WIDTH
D_FF = 4 * D_MODEL
DILATION_CONFIGS = ((128, 1), (512, 4), (2048, 16))
BLOCK = 128
ROPE_THETA = 10000.0
LN_EPS = 1e-5
SUBLN_EPS = 1e-5
NEG = -1e30
DEEPNORM_ALPHA = (2.0 * DEPTH) ** 0.25
DEEPNORM_BETA = (8.0 * DEPTH) ** -0.25

kernel_name = "hymba_diffattn_dilated_sqrelu_deepnorm"


def rope_tables(seq, dim):
    inv = 1.0 / (ROPE_THETA ** (jnp.arange(0, dim, 2, dtype=jnp.float32) / dim))
    ang = jnp.arange(seq, dtype=jnp.float32)[:, None] * inv[None, :]
    ang = jnp.concatenate([ang, ang], axis=-1)
    return jnp.cos(ang), jnp.sin(ang)


def apply_rope(x, cos, sin):
    half = x.shape[-1] // 2
    rot = jnp.concatenate([-x[..., half:], x[..., :half]], axis=-1)
    shape = (1, cos.shape[0]) + (1,) * (x.ndim - 3) + (cos.shape[1],)
    return x * cos.reshape(shape) + rot * sin.reshape(shape)


def layer_norm(x, g, b):
    xf = x.astype(jnp.float32)
    mu = jnp.mean(xf, axis=-1, keepdims=True)
    var = jnp.mean(jnp.square(xf - mu), axis=-1, keepdims=True)
    y = (xf - mu) * lax.rsqrt(var + LN_EPS) * g.astype(jnp.float32) + b.astype(jnp.float32)
    return y.astype(x.dtype)


def diff_attention(q, k, v, lam, subln_g, lambda_init):
    B, S, H, _, Dqk = q.shape
    nb = S // BLOCK
    scale = Dqk ** -0.5
    qb = q.reshape(B, nb, BLOCK, H, 2, Dqk).transpose(1, 0, 2, 3, 4, 5)
    k_pos = jnp.arange(S)

    def one_block(args):
        qi, i = args
        s = jnp.einsum('bqhmd,bkhmd->bhmqk', qi, k) * scale
        q_pos = i * BLOCK + jnp.arange(BLOCK)
        causal = k_pos[None, :] <= q_pos[:, None]
        p = jax.nn.softmax(jnp.where(causal, s, NEG), axis=-1)
        a = p[:, :, 0] - lam * p[:, :, 1]
        return jnp.einsum('bhqk,bkhd->bqhd', a, v)

    o = lax.map(one_block, (qb, jnp.arange(nb)))
    o = o.transpose(1, 0, 2, 3, 4).reshape(B, S, H, v.shape[-1])
    o = o * lax.rsqrt(jnp.mean(o * o, axis=-1, keepdims=True) + SUBLN_EPS) * subln_g.astype(jnp.float32)
    return o * (1.0 - lambda_init)


def dilated_window_attention(q, k, v, window, dilation):
    B, S, H, D = q.shape
    steps = window // dilation
    L = S // dilation
    Lp = ((L + BLOCK - 1) // BLOCK) * BLOCK
    nb = Lp // BLOCK

    def to_sub(t):
        t = t.reshape(B, L, dilation, H, D).transpose(0, 2, 3, 1, 4)
        t = jnp.pad(t, ((0, 0), (0, 0), (0, 0), (0, Lp - L), (0, 0)))
        return t.reshape(B, dilation, H, nb, BLOCK, D)

    qb, kb, vb = to_sub(q), to_sub(k), to_sub(v)
    pad_prev = ((0, 0), (0, 0), (0, 0), (1, 0), (0, 0), (0, 0))
    kwin = jnp.concatenate([jnp.pad(kb[:, :, :, :-1], pad_prev), kb], axis=4)
    vwin = jnp.concatenate([jnp.pad(vb[:, :, :, :-1], pad_prev), vb], axis=4)

    s = jnp.einsum('brhnqd,brhnkd->brhnqk', qb, kwin) * (D ** -0.5)
    qi = jnp.arange(BLOCK)[:, None]
    kj = jnp.arange(2 * BLOCK)[None, :]
    dist = qi + BLOCK - kj
    band = (dist >= 0) & (dist <= steps)
    key_exists = (jnp.arange(nb)[:, None, None] * BLOCK + kj[None] - BLOCK) >= 0
    mask = band[None] & key_exists
    s = jnp.where(mask, s, NEG)
    m = jnp.max(s, axis=-1, keepdims=True)
    e = jnp.exp(s - m)
    l = jnp.sum(e, axis=-1, keepdims=True)
    o = jnp.einsum('brhnqk,brhnkd->brhnqd', e, vwin) / l
    lse = m + jnp.log(l)

    def from_sub(t):
        c = t.shape[-1]
        t = t.reshape(B, dilation, H, Lp, c)[:, :, :, :L]
        return t.transpose(0, 3, 1, 2, 4).reshape(B, S, H, c)

    return from_sub(o), from_sub(lse)


def dilated_attention(q, k, v):
    outs, lses = [], []
    for window, dilation in DILATION_CONFIGS:
        o, lse = dilated_window_attention(q, k, v, window, dilation)
        outs.append(o)
        lses.append(lse)
    w = jax.nn.softmax(jnp.stack(lses, axis=0), axis=0)
    return jnp.sum(w * jnp.stack(outs, axis=0), axis=0)


def setup_inputs(seed: int = 0) -> dict:
    key = jax.random.key(seed)
    ks = jax.random.split(key, 16)
    f32 = jnp.float32
    x = jax.random.normal(ks[0], (BATCH, SEQ, D_MODEL), f32)
    w_in = jax.random.normal(ks[1], (DEPTH, D_MODEL, IN_COLS), f32) * D_MODEL ** -0.5
    col_scale = jnp.concatenate([
        jnp.ones((2 * DIFF_WIDTH,), f32), jnp.full((DIFF_WIDTH,), DEEPNORM_BETA, f32),
        jnp.ones((2 * DIL_WIDTH,), f32), jnp.full((DIL_WIDTH,), DEEPNORM_BETA, f32)])
    w_in = w_in * col_scale
    lambda_q1 = 0.1 * jax.random.normal(ks[2], (DEPTH, DIFF_QK_DIM), f32)
    lambda_k1 = 0.1 * jax.random.normal(ks[3], (DEPTH, DIFF_QK_DIM), f32)
    lambda_q2 = 0.1 * jax.random.normal(ks[4], (DEPTH, DIFF_QK_DIM), f32)
    lambda_k2 = 0.1 * jax.random.normal(ks[5], (DEPTH, DIFF_QK_DIM), f32)
    subln_g = 1.0 + 0.02 * jax.random.normal(ks[6], (DEPTH, HEAD_DIM), f32)
    w_out = jax.random.normal(ks[7], (DEPTH, MIX_WIDTH, D_MODEL), f32) * (MIX_WIDTH ** -0.5 * DEEPNORM_BETA)
    ln1_g = 1.0 + 0.02 * jax.random.normal(ks[8], (DEPTH, D_MODEL), f32)
    ln1_b = 0.02 * jax.random.normal(ks[9], (DEPTH, D_MODEL), f32)
    w_ff1 = jax.random.normal(ks[10], (DEPTH, D_MODEL, D_FF), f32) * (D_MODEL ** -0.5 * DEEPNORM_BETA)
    w_ff2 = jax.random.normal(ks[11], (DEPTH, D_FF, D_MODEL), f32) * (D_FF ** -0.5 * DEEPNORM_BETA)
    ln2_g = 1.0 + 0.02 * jax.random.normal(ks[12], (DEPTH, D_MODEL), f32)
    ln2_b = 0.02 * jax.random.normal(ks[13], (DEPTH, D_MODEL), f32)
    return {"x": x, "w_in": w_in, "lambda_q1": lambda_q1, "lambda_k1": lambda_k1,
            "lambda_q2": lambda_q2, "lambda_k2": lambda_k2, "subln_g": subln_g,
            "w_out": w_out, "ln1_g": ln1_g, "ln1_b": ln1_b, "w_ff1": w_ff1,
            "w_ff2": w_ff2, "ln2_g": ln2_g, "ln2_b": ln2_b}


def reference(x, w_in, lambda_q1, lambda_k1, lambda_q2, lambda_k2, subln_g, w_out,
              ln1_g, ln1_b, w_ff1, w_ff2, ln2_g, ln2_b):
    B, S, _ = x.shape
    f32 = jnp.float32
    cos_a, sin_a = rope_tables(S, DIFF_QK_DIM)
    cos_b, sin_b = rope_tables(S, HEAD_DIM)
    splits = [DIFF_WIDTH, 2 * DIFF_WIDTH, 3 * DIFF_WIDTH,
              3 * DIFF_WIDTH + DIL_WIDTH, 3 * DIFF_WIDTH + 2 * DIL_WIDTH]
    for l in range(DEPTH):
        lambda_init = 0.8 - 0.6 * math.exp(-0.3 * l)
        proj = jnp.einsum('bsd,dc->bsc', x, w_in[l]).astype(f32)
        q_a, k_a, v_a, q_b, k_b, v_b = jnp.split(proj, splits, axis=-1)
        q_a = apply_rope(q_a.reshape(B, S, N_HEADS_DIFF, 2, DIFF_QK_DIM), cos_a, sin_a)
        k_a = apply_rope(k_a.reshape(B, S, N_HEADS_DIFF, 2, DIFF_QK_DIM), cos_a, sin_a)
        v_a = v_a.reshape(B, S, N_HEADS_DIFF, HEAD_DIM)
        lam = (jnp.exp(jnp.sum(lambda_q1[l].astype(f32) * lambda_k1[l].astype(f32)))
               - jnp.exp(jnp.sum(lambda_q2[l].astype(f32) * lambda_k2[l].astype(f32)))
               + lambda_init)
        o_a = diff_attention(q_a, k_a, v_a, lam, subln_g[l], lambda_init)
        q_b = apply_rope(q_b.reshape(B, S, N_HEADS_DIL, HEAD_DIM), cos_b, sin_b)
        k_b = apply_rope(k_b.reshape(B, S, N_HEADS_DIL, HEAD_DIM), cos_b, sin_b)
        v_b = v_b.reshape(B, S, N_HEADS_DIL, HEAD_DIM)
        o_b = dilated_attention(q_b, k_b, v_b)
        mix = jnp.concatenate([o_a.reshape(B, S, DIFF_WIDTH),
                               o_b.reshape(B, S, DIL_WIDTH)], axis=-1).astype(x.dtype)
        attn_out = jnp.einsum('bsc,cd->bsd', mix, w_out[l])
        x = layer_norm(DEEPNORM_ALPHA * x + attn_out, ln1_g[l], ln1_b[l])
        hdn = jnp.square(jax.nn.relu(jnp.einsum('bsd,df->bsf', x, w_ff1[l])))
        ff_out = jnp.einsum('bsf,fd->bsd', hdn, w_ff2[l])
        x = layer_norm(DEEPNORM_ALPHA * x + ff_out, ln2_g[l], ln2_b[l])
    return x
```

```python
import functools
import math

import numpy as np
import jax
import jax.numpy as jnp
from jax import lax
from jax.experimental import pallas as pl
from jax.experimental.pallas import tpu as pltpu

F32 = jnp.float32
BF16 = jnp.bfloat16

HEAD_DIM = 128
DIFF_QK_DIM = 64
ROPE_THETA = 10000.0
LN_EPS = 1e-5
SUBLN_EPS = 1e-5
NEG = -1e30
LOG2E = 1.4426950408889634
DILATION_CONFIGS = ((128, 1), (512, 4), (2048, 16))

V7X_LANES = 128
VMEM_LIMIT = 56 * 1024 * 1024


def _proj_kernel(x_ref, w_ref, cosa_ref, sina_ref, cosb_ref, sinb_ref, o_ref, xb_ref, acc_ref,
                 *, n_heads_per_tile, qa_scale, qb_scale):
    j = pl.program_id(1)

    @pl.when(j == 0)
    def _():
        xb_ref[...] = x_ref[...].astype(BF16)

    acc_ref[...] = jnp.dot(xb_ref[...], w_ref[...], preferred_element_type=F32)

    def rope_a(scale):
        cos = cosa_ref[...]
        sin = sina_ref[...]
        lane = lax.broadcasted_iota(jnp.int32, cos.shape, 1)
        first_half = (lane % DIFF_QK_DIM) < (DIFF_QK_DIM // 2)
        for h in range(n_heads_per_tile):
            sl = slice(h * HEAD_DIM, (h + 1) * HEAD_DIM)
            t = acc_ref[:, sl]
            rot = jnp.where(first_half, pltpu.roll(t, 96, 1), pltpu.roll(t, 32, 1))
            y = t * cos + rot * sin
            if scale != 1.0:
                y = y * scale
            o_ref[:, sl] = y.astype(o_ref.dtype)

    def rope_b(scale):
        cos = cosb_ref[...]
        sin = sinb_ref[...]
        for h in range(n_heads_per_tile):
            sl = slice(h * HEAD_DIM, (h + 1) * HEAD_DIM)
            t = acc_ref[:, sl]
            y = t * cos + pltpu.roll(t, 64, 1) * sin
            if scale != 1.0:
                y = y * scale
            o_ref[:, sl] = y.astype(o_ref.dtype)

    @pl.when(j == 0)
    def _():
        rope_a(qa_scale)

    @pl.when(j == 1)
    def _():
        rope_a(1.0)

    @pl.when(j == 3)
    def _():
        rope_b(qb_scale)

    @pl.when(j == 4)
    def _():
        rope_b(1.0)

    @pl.when((j == 2) | (j == 5))
    def _():
        o_ref[...] = acc_ref[...].astype(o_ref.dtype)


def _rope_tables(seq):
    def ang(dim):
        inv = 1.0 / (ROPE_THETA ** (jnp.arange(0, dim, 2, dtype=F32) / dim))
        return jnp.arange(seq, dtype=F32)[:, None] * inv[None, :]
    a = ang(DIFF_QK_DIM)
    ca, sa = jnp.cos(a), jnp.sin(a)
    cos_a = jnp.concatenate([ca, ca, ca, ca], axis=-1)
    sin_a = jnp.concatenate([-sa, sa, -sa, sa], axis=-1)
    b = ang(HEAD_DIM)
    cb, sb = jnp.cos(b), jnp.sin(b)
    cos_b = jnp.concatenate([cb, cb], axis=-1)
    sin_b = jnp.concatenate([-sb, sb], axis=-1)
    return cos_a, sin_a, cos_b, sin_b


def _projection(x2d, w_bf16, seq, *, tm=1024, tn=1024):
    m, k = x2d.shape
    n = w_bf16.shape[1]
    assert m % tm == 0 and n % tn == 0 and seq % tm == 0 and tn % HEAD_DIM == 0
    assert n // tn == 6, "column tiles must line up with the q_a|k_a|v_a|q_b|k_b|v_b groups"
    cos_a, sin_a, cos_b, sin_b = _rope_tables(seq)
    pos_blocks = seq // tm
    tab_spec = pl.BlockSpec((tm, HEAD_DIM), lambda i, j: (i % pos_blocks, 0))
    kern = functools.partial(
        _proj_kernel, n_heads_per_tile=tn // HEAD_DIM,
        qa_scale=DIFF_QK_DIM ** -0.5 * LOG2E, qb_scale=HEAD_DIM ** -0.5 * LOG2E)
    return pl.pallas_call(
        kern,
        grid=(m // tm, n // tn),
        in_specs=[pl.BlockSpec((tm, k), lambda i, j: (i, 0)),
                  pl.BlockSpec((k, tn), lambda i, j: (0, j)),
                  tab_spec, tab_spec, tab_spec, tab_spec],
        out_specs=pl.BlockSpec((tm, tn), lambda i, j: (i, j)),
        out_shape=jax.ShapeDtypeStruct((m, n), BF16),
        scratch_shapes=[pltpu.VMEM((tm, k), BF16), pltpu.VMEM((tm, tn), F32)],
        compiler_params=pltpu.CompilerParams(
            dimension_semantics=("parallel", "arbitrary"), vmem_limit_bytes=VMEM_LIMIT),
        name="proj_rope",
    )(x2d, w_bf16, cos_a, sin_a, cos_b, sin_b)


def _diff_attn_kernel(lam_ref, g_ref, q_ref, k_ref, v_ref, o_ref, lam_sc, *, tq, tk, lambda_init):
    qi = pl.program_id(2)

    @pl.when(qi == 0)
    def _():
        lv = lam_ref[...]
        s1 = jnp.sum(lv[0:1] * lv[1:2], axis=-1, keepdims=True)
        s2 = jnp.sum(lv[2:3] * lv[3:4], axis=-1, keepdims=True)
        lam = jnp.exp(s1) - jnp.exp(s2) + lambda_init
        lam_sc[...] = jnp.broadcast_to(lam, lam_sc.shape)

    q = q_ref[0]
    lane = lax.broadcasted_iota(jnp.int32, q.shape, 1)
    zero = jnp.zeros_like(q)
    q2 = jnp.concatenate([jnp.where(lane < DIFF_QK_DIM, q, zero),
                          jnp.where(lane >= DIFF_QK_DIM, q, zero)], axis=0)

    def step(j, carry, masked):
        m, l, acc = carry
        kb = k_ref[0, pl.ds(pl.multiple_of(j * tk, tk), tk), :]
        vb = v_ref[0, pl.ds(pl.multiple_of(j * tk, tk), tk), :]
        s = lax.dot_general(q2, kb, (((1,), (1,)), ((), ())), preferred_element_type=F32)
        if masked:
            row = lax.broadcasted_iota(jnp.int32, s.shape, 0)
            col = lax.broadcasted_iota(jnp.int32, s.shape, 1)
            q_pos = qi * tq + jnp.where(row >= tq, row - tq, row)
            s = jnp.where(j * tk + col <= q_pos, s, NEG)
        m_new = jnp.maximum(m, jnp.max(s, axis=-1, keepdims=True))
        alpha = jnp.exp2(m - m_new)
        p = jnp.exp2(s - m_new)
        l = alpha * l + jnp.sum(p, axis=-1, keepdims=True)
        acc = alpha * acc + jnp.dot(p.astype(BF16), vb, preferred_element_type=F32)
        return m_new, l, acc

    init = (jnp.full((2 * tq, 1), NEG, F32), jnp.zeros((2 * tq, 1), F32),
            jnp.zeros((2 * tq, HEAD_DIM), F32))
    n_full = (qi * tq) // tk
    carry = lax.fori_loop(0, n_full, lambda j, c: step(j, c, False), init)
    m, l, acc = step(n_full, carry, True)

    o = acc / l
    lam = lam_sc[0:1, :]
    d = o[:tq] - lam * o[tq:]
    d = d * lax.rsqrt(jnp.mean(d * d, axis=-1, keepdims=True) + SUBLN_EPS) * g_ref[...]
    o_ref[0] = (d * (1.0 - lambda_init)).astype(o_ref.dtype)


def _diff_attention(proj, lam_vecs, subln_g, *, n_heads, col0_q, col0_k, col0_v, lambda_init,
                    tq=128, tk=256):
    b, s, _ = proj.shape
    assert s % tq == 0 and s % tk == 0 and tk % tq == 0
    kern = functools.partial(_diff_attn_kernel, tq=tq, tk=tk, lambda_init=lambda_init)
    return pl.pallas_call(
        kern,
        grid=(b, n_heads, s // tq),
        in_specs=[pl.BlockSpec(lam_vecs.shape, lambda bi, h, qi: (0, 0)),
                  pl.BlockSpec((1, HEAD_DIM), lambda bi, h, qi: (0, 0)),
                  pl.BlockSpec((1, tq, HEAD_DIM), lambda bi, h, qi: (bi, qi, col0_q + h)),
                  pl.BlockSpec((1, s, HEAD_DIM), lambda bi, h, qi: (bi, 0, col0_k + h)),
                  pl.BlockSpec((1, s, HEAD_DIM), lambda bi, h, qi: (bi, 0, col0_v + h))],
        out_specs=pl.BlockSpec((1, tq, HEAD_DIM), lambda bi, h, qi: (bi, qi, h)),
        out_shape=jax.ShapeDtypeStruct((b, s, n_heads * HEAD_DIM), BF16),
        scratch_shapes=[pltpu.VMEM((8, HEAD_DIM), F32)],
        compiler_params=pltpu.CompilerParams(
            dimension_semantics=("parallel", "parallel", "arbitrary"), vmem_limit_bytes=VMEM_LIMIT),
        name="diff_attn",
    )(lam_vecs, subln_g, proj, proj, proj)


def _dilated_bias_tiles(tq, tk):
    max_finite_window = max(w for w, _ in DILATION_CONFIGS[:-1])
    n_near = (max_finite_window + tk - 1) // tq + 1
    iq = np.arange(tq)[:, None]
    ik = np.arange(tk)[None, :]
    tiles = []
    for delta in range(n_near + 1):
        dist = delta * tq + iq - ik
        count = np.zeros(dist.shape, np.int64)
        for window, dilation in DILATION_CONFIGS:
            count += (dist >= 0) & (dist <= window) & (dist % dilation == 0)
        tiles.append(np.where(count > 0, np.log2(np.maximum(count, 1)), NEG))
    return np.stack(tiles).astype(np.float32), n_near


def _dil_attn_kernel(bias_ref, q_ref, k_ref, v_ref, o_ref, *, tq, tk, n_near):
    qi = pl.program_id(2)
    q = q_ref[0]
    ratio = tk // tq

    def step(j, carry):
        m, l, acc = carry
        kb = k_ref[0, pl.ds(pl.multiple_of(j * tk, tk), tk), :]
        vb = v_ref[0, pl.ds(pl.multiple_of(j * tk, tk), tk), :]
        s = lax.dot_general(q, kb, (((1,), (1,)), ((), ())), preferred_element_type=F32)
        s = s + bias_ref[jnp.minimum(qi - j * ratio, n_near)]
        m_new = jnp.maximum(m, jnp.max(s, axis=-1, keepdims=True))
        alpha = jnp.exp2(m - m_new)
        p = jnp.exp2(s - m_new)
        l = alpha * l + jnp.sum(p, axis=-1, keepdims=True)
        acc = alpha * acc + jnp.dot(p.astype(BF16), vb, preferred_element_type=F32)
        return m_new, l, acc

    init = (jnp.full((tq, 1), NEG, F32), jnp.zeros((tq, 1), F32), jnp.zeros((tq, HEAD_DIM), F32))
    n_blocks = (qi * tq) // tk + 1
    m, l, acc = lax.fori_loop(0, n_blocks, step, init)
    o_ref[0] = (acc / l).astype(o_ref.dtype)


def _dilated_attention(proj, *, n_heads, col0_q, col0_k, col0_v, tq=128, tk=256):
    b, s, _ = proj.shape
    assert s % tq == 0 and s % tk == 0 and tk % tq == 0
    assert all(tq % d == 0 for _, d in DILATION_CONFIGS) and DILATION_CONFIGS[-1][0] >= s
    bias_np, n_near = _dilated_bias_tiles(tq, tk)
    bias = jnp.asarray(bias_np)
    kern = functools.partial(_dil_attn_kernel, tq=tq, tk=tk, n_near=n_near)
    return pl.pallas_call(
        kern,
        grid=(b, n_heads, s // tq),
        in_specs=[pl.BlockSpec(bias.shape, lambda bi, h, qi: (0, 0, 0)),
                  pl.BlockSpec((1, tq, HEAD_DIM), lambda bi, h, qi: (bi, qi, col0_q + h)),
                  pl.BlockSpec((1, s, HEAD_DIM), lambda bi, h, qi: (bi, 0, col0_k + h)),
                  pl.BlockSpec((1, s, HEAD_DIM), lambda bi, h, qi: (bi, 0, col0_v + h))],
        out_specs=pl.BlockSpec((1, tq, HEAD_DIM), lambda bi, h, qi: (bi, qi, h)),
        out_shape=jax.ShapeDtypeStruct((b, s, n_heads * HEAD_DIM), BF16),
        compiler_params=pltpu.CompilerParams(
            dimension_semantics=("parallel", "parallel", "arbitrary"), vmem_limit_bytes=VMEM_LIMIT),
        name="dilated_attn",
    )(bias, proj, proj, proj)


def _layer_norm_rows(y, g, b):
    mu = jnp.mean(y, axis=-1, keepdims=True)
    yc = y - mu
    var = jnp.mean(yc * yc, axis=-1, keepdims=True)
    return yc * lax.rsqrt(var + LN_EPS) * g + b


def _out_ln_kernel(oa_ref, ob_ref, wa_ref, wb_ref, x_ref, g_ref, b_ref, y_ref, *, alpha):
    attn = jnp.dot(oa_ref[...], wa_ref[...], preferred_element_type=F32)
    attn = attn + jnp.dot(ob_ref[...], wb_ref[...], preferred_element_type=F32)
    y = alpha * x_ref[...] + attn
    y_ref[...] = _layer_norm_rows(y, g_ref[...], b_ref[...])


def _out_proj_ln(o_a, o_b, w_out_bf16, x2d, g, b, *, alpha, tm=512):
    m, d = x2d.shape
    ka, kb = o_a.shape[1], o_b.shape[1]
    assert m % tm == 0 and w_out_bf16.shape == (ka + kb, d) and ka == kb
    const = dict(pipeline_mode=pl.Buffered(1))
    return pl.pallas_call(
        functools.partial(_out_ln_kernel, alpha=alpha),
        grid=(m // tm,),
        in_specs=[pl.BlockSpec((tm, ka), lambda i: (i, 0)),
                  pl.BlockSpec((tm, kb), lambda i: (i, 0)),
                  pl.BlockSpec((ka, d), lambda i: (0, 0), **const),
                  pl.BlockSpec((kb, d), lambda i: (1, 0), **const),
                  pl.BlockSpec((tm, d), lambda i: (i, 0)),
                  pl.BlockSpec((1, d), lambda i: (0, 0)),
                  pl.BlockSpec((1, d), lambda i: (0, 0))],
        out_specs=pl.BlockSpec((tm, d), lambda i: (i, 0)),
        out_shape=jax.ShapeDtypeStruct((m, d), F32),
        compiler_params=pltpu.CompilerParams(
            dimension_semantics=("parallel",), vmem_limit_bytes=VMEM_LIMIT),
        name="out_proj_ln",
    )(o_a, o_b, w_out_bf16, w_out_bf16, x2d, g, b)


def _ffn_kernel(x_ref, w1_ref, w2_ref, g_ref, b_ref, o_ref, xb_ref, acc_ref, *, alpha):
    f = pl.program_id(1)

    @pl.when(f == 0)
    def _():
        xb_ref[...] = x_ref[...].astype(BF16)

    h = jnp.dot(xb_ref[...], w1_ref[...], preferred_element_type=F32)
    h = jnp.square(jnp.maximum(h, 0.0)).astype(BF16)
    part = jnp.dot(h, w2_ref[...], preferred_element_type=F32)

    @pl.when(f == 0)
    def _():
        acc_ref[...] = part

    @pl.when(f > 0)
    def _():
        acc_ref[...] += part

    @pl.when(f == pl.num_programs(1) - 1)
    def _():
        y = alpha * x_ref[...] + acc_ref[...]
        o_ref[...] = _layer_norm_rows(y, g_ref[...], b_ref[...])


def _ffn_ln(x2d, w1_bf16, w2_bf16, g, b, *, alpha, tm=512, tf=1024):
    m, d = x2d.shape
    dff = w1_bf16.shape[1]
    assert m % tm == 0 and dff % tf == 0
    return pl.pallas_call(
        functools.partial(_ffn_kernel, alpha=alpha),
        grid=(m // tm, dff // tf),
        in_specs=[pl.BlockSpec((tm, d), lambda i, f: (i, 0)),
                  pl.BlockSpec((d, tf), lambda i, f: (0, f)),
                  pl.BlockSpec((tf, d), lambda i, f: (f, 0)),
                  pl.BlockSpec((1, d), lambda i, f: (0, 0)),
                  pl.BlockSpec((1, d), lambda i, f: (0, 0))],
        out_specs=pl.BlockSpec((tm, d), lambda i, f: (i, 0)),
        out_shape=jax.ShapeDtypeStruct((m, d), F32),
        scratch_shapes=[pltpu.VMEM((tm, d), BF16), pltpu.VMEM((tm, d), F32)],
        compiler_params=pltpu.CompilerParams(
            dimension_semantics=("parallel", "arbitrary"), vmem_limit_bytes=VMEM_LIMIT),
        name="ffn_ln",
    )(x2d, w1_bf16, w2_bf16, g, b)


def kernel(x, w_in, lambda_q1, lambda_k1, lambda_q2, lambda_k2, subln_g, w_out, ln1_g, ln1_b,
           w_ff1, w_ff2, ln2_g, ln2_b):
    bsz, seq, d_model = x.shape
    depth = w_in.shape[0]
    n_heads = d_model // (2 * HEAD_DIM)
    width = n_heads * HEAD_DIM
    alpha = (2.0 * depth) ** 0.25
    blocks_per_group = width // HEAD_DIM

    h2d = x.reshape(bsz * seq, d_model)
    for l in range(depth):
        lambda_init = 0.8 - 0.6 * math.exp(-0.3 * l)
        proj = _projection(h2d, w_in[l].astype(BF16), seq).reshape(bsz, seq, 6 * width)
        lam_vecs = jnp.stack([lambda_q1[l], lambda_k1[l], lambda_q2[l], lambda_k2[l]]).astype(F32)
        o_a = _diff_attention(proj, lam_vecs, subln_g[l].astype(F32).reshape(1, HEAD_DIM),
                              n_heads=n_heads, col0_q=0, col0_k=blocks_per_group,
                              col0_v=2 * blocks_per_group, lambda_init=lambda_init)
        o_b = _dilated_attention(proj, n_heads=n_heads, col0_q=3 * blocks_per_group,
                                 col0_k=4 * blocks_per_group, col0_v=5 * blocks_per_group)
        h2d = _out_proj_ln(o_a.reshape(bsz * seq, width), o_b.reshape(bsz * seq, width),
                           w_out[l].astype(BF16), h2d,
                           ln1_g[l].reshape(1, d_model), ln1_b[l].reshape(1, d_model), alpha=alpha)
        h2d = _ffn_ln(h2d, w_ff1[l].astype(BF16), w_ff2[l].astype(BF16),
                      ln2_g[l].reshape(1, d_model), ln2_b[l].reshape(1, d_model), alpha=alpha)
    return h2d.reshape(bsz, seq, d_model)
```

```python
import functools
import math

import numpy as np
import jax
import jax.numpy as jnp
from jax import lax
from jax.experimental import pallas as pl
from jax.experimental.pallas import tpu as pltpu

F32 = jnp.float32
BF16 = jnp.bfloat16

HEAD_DIM = 128
DIFF_QK_DIM = 64
ROPE_THETA = 10000.0
LN_EPS = 1e-5
SUBLN_EPS = 1e-5
NEG = -1e30
LOG2E = 1.4426950408889634
DILATION_CONFIGS = ((128, 1), (512, 4), (2048, 16))

V7X_LANES = 128
VMEM_LIMIT = 56 * 1024 * 1024


def _proj_kernel(x_ref, w_ref, cosa_ref, sina_ref, cosb_ref, sinb_ref, o_ref, xb_ref, acc_ref,
                 *, n_heads_per_tile, qa_scale, qb_scale):
    j = pl.program_id(1)

    @pl.when(j == 0)
    def _():
        xb_ref[...] = x_ref[...].astype(BF16)

    acc_ref[...] = jnp.dot(xb_ref[...], w_ref[...], preferred_element_type=F32)

    def rope_a(scale):
        cos = cosa_ref[...]
        sin = sina_ref[...]
        lane = lax.broadcasted_iota(jnp.int32, cos.shape, 1)
        first_half = (lane % DIFF_QK_DIM) < (DIFF_QK_DIM // 2)
        for h in range(n_heads_per_tile):
            sl = slice(h * HEAD_DIM, (h + 1) * HEAD_DIM)
            t = acc_ref[:, sl]
            rot = jnp.where(first_half, pltpu.roll(t, 96, 1), pltpu.roll(t, 32, 1))
            y = t * cos + rot * sin
            if scale != 1.0:
                y = y * scale
            o_ref[:, sl] = y.astype(o_ref.dtype)

    def rope_b(scale):
        cos = cosb_ref[...]
        sin = sinb_ref[...]
        for h in range(n_heads_per_tile):
            sl = slice(h * HEAD_DIM, (h + 1) * HEAD_DIM)
            t = acc_ref[:, sl]
            y = t * cos + pltpu.roll(t, 64, 1) * sin
            if scale != 1.0:
                y = y * scale
            o_ref[:, sl] = y.astype(o_ref.dtype)

    @pl.when(j == 0)
    def _():
        rope_a(qa_scale)

    @pl.when(j == 1)
    def _():
        rope_a(1.0)

    @pl.when(j == 3)
    def _():
        rope_b(qb_scale)

    @pl.when(j == 4)
    def _():
        rope_b(1.0)

    @pl.when((j == 2) | (j == 5))
    def _():
        o_ref[...] = acc_ref[...].astype(o_ref.dtype)


def _rope_tables(seq):
    def ang(dim):
        inv = 1.0 / (ROPE_THETA ** (jnp.arange(0, dim, 2, dtype=F32) / dim))
        return jnp.arange(seq, dtype=F32)[:, None] * inv[None, :]
    a = ang(DIFF_QK_DIM)
    ca, sa = jnp.cos(a), jnp.sin(a)
    cos_a = jnp.concatenate([ca, ca, ca, ca], axis=-1)
    sin_a = jnp.concatenate([-sa, sa, -sa, sa], axis=-1)
    b = ang(HEAD_DIM)
    cb, sb = jnp.cos(b), jnp.sin(b)
    cos_b = jnp.concatenate([cb, cb], axis=-1)
    sin_b = jnp.concatenate([-sb, sb], axis=-1)
    return cos_a, sin_a, cos_b, sin_b


def _projection(x2d, w_bf16, seq, *, tm=1024, tn=1024):
    m, k = x2d.shape
    n = w_bf16.shape[1]
    assert m % tm == 0 and n % tn == 0 and seq % tm == 0 and tn % HEAD_DIM == 0
    assert n // tn == 6, "column tiles must line up with the q_a|k_a|v_a|q_b|k_b|v_b groups"
    cos_a, sin_a, cos_b, sin_b = _rope_tables(seq)
    pos_blocks = seq // tm
    tab_spec = pl.BlockSpec((tm, HEAD_DIM), lambda i, j: (i % pos_blocks, 0))
    kern = functools.partial(
        _proj_kernel, n_heads_per_tile=tn // HEAD_DIM,
        qa_scale=DIFF_QK_DIM ** -0.5 * LOG2E, qb_scale=HEAD_DIM ** -0.5 * LOG2E)
    return pl.pallas_call(
        kern,
        grid=(m // tm, n // tn),
        in_specs=[pl.BlockSpec((tm, k), lambda i, j: (i, 0)),
                  pl.BlockSpec((k, tn), lambda i, j: (0, j)),
                  tab_spec, tab_spec, tab_spec, tab_spec],
        out_specs=pl.BlockSpec((tm, tn), lambda i, j: (i, j)),
        out_shape=jax.ShapeDtypeStruct((m, n), BF16),
        scratch_shapes=[pltpu.VMEM((tm, k), BF16), pltpu.VMEM((tm, tn), F32)],
        compiler_params=pltpu.CompilerParams(
            dimension_semantics=("parallel", "arbitrary"), vmem_limit_bytes=VMEM_LIMIT),
        name="proj_rope",
    )(x2d, w_bf16, cos_a, sin_a, cos_b, sin_b)


def _diff_attn_kernel(lam_ref, g_ref, q_ref, k_ref, v_ref, o_ref, *, tq, lambda_init):
    seq = q_ref.shape[1]
    lv = lam_ref[...]
    lam = (jnp.exp(jnp.sum(lv[0:1] * lv[1:2], axis=-1, keepdims=True))
           - jnp.exp(jnp.sum(lv[2:3] * lv[3:4], axis=-1, keepdims=True)) + lambda_init)
    g = g_ref[...]

    lane = lax.broadcasted_iota(jnp.int32, (tq, HEAD_DIM), 1)
    row = lax.broadcasted_iota(jnp.int32, (2 * tq, tq), 0)
    col = lax.broadcasted_iota(jnp.int32, (2 * tq, tq), 1)
    visible = col <= jnp.where(row >= tq, row - tq, row)

    for qb in range(seq // tq):
        lo, hi = qb * tq, (qb + 1) * tq
        q = q_ref[0, lo:hi, :]
        zero = jnp.zeros_like(q)
        q2 = jnp.concatenate([jnp.where(lane < DIFF_QK_DIM, q, zero),
                              jnp.where(lane >= DIFF_QK_DIM, q, zero)], axis=0)
        s = lax.dot_general(q2, k_ref[0, :hi, :], (((1,), (1,)), ((), ())),
                            preferred_element_type=F32)
        s_diag = jnp.where(visible, s[:, lo:], NEG)
        m = jnp.max(s_diag, axis=-1, keepdims=True)
        if qb > 0:
            m = jnp.maximum(m, jnp.max(s[:, :lo], axis=-1, keepdims=True))
        p_diag = jnp.exp2(s_diag - m)
        l = jnp.sum(p_diag, axis=-1, keepdims=True)
        pv = jnp.dot(p_diag.astype(BF16), v_ref[0, lo:hi, :], preferred_element_type=F32)
        if qb > 0:
            p_main = jnp.exp2(s[:, :lo] - m)
            l = l + jnp.sum(p_main, axis=-1, keepdims=True)
            pv = pv + jnp.dot(p_main.astype(BF16), v_ref[0, :lo, :], preferred_element_type=F32)
        o = pv / l
        d = o[:tq] - lam * o[tq:]
        d = d * lax.rsqrt(jnp.mean(d * d, axis=-1, keepdims=True) + SUBLN_EPS) * g
        o_ref[0, lo:hi, :] = (d * (1.0 - lambda_init)).astype(o_ref.dtype)


def _diff_attention(proj, lam_vecs, subln_g, *, n_heads, col0_q, col0_k, col0_v, lambda_init, tq=256):
    b, s, _ = proj.shape
    assert s % tq == 0
    kern = functools.partial(_diff_attn_kernel, tq=tq, lambda_init=lambda_init)
    return pl.pallas_call(
        kern,
        grid=(b, n_heads),
        in_specs=[pl.BlockSpec(lam_vecs.shape, lambda bi, h: (0, 0)),
                  pl.BlockSpec((1, HEAD_DIM), lambda bi, h: (0, 0)),
                  pl.BlockSpec((1, s, HEAD_DIM), lambda bi, h: (bi, 0, col0_q + h)),
                  pl.BlockSpec((1, s, HEAD_DIM), lambda bi, h: (bi, 0, col0_k + h)),
                  pl.BlockSpec((1, s, HEAD_DIM), lambda bi, h: (bi, 0, col0_v + h))],
        out_specs=pl.BlockSpec((1, s, HEAD_DIM), lambda bi, h: (bi, 0, h)),
        out_shape=jax.ShapeDtypeStruct((b, s, n_heads * HEAD_DIM), BF16),
        compiler_params=pltpu.CompilerParams(
            dimension_semantics=("parallel", "parallel"), vmem_limit_bytes=VMEM_LIMIT),
        name="diff_attn",
    )(lam_vecs, subln_g, proj, proj, proj)


def _dilated_bias_table(seq, tq):
    dist = np.arange(tq)[:, None] + (seq - tq) - np.arange(seq)[None, :]
    count = np.zeros(dist.shape, np.int64)
    for window, dilation in DILATION_CONFIGS:
        count += (dist >= 0) & (dist <= window) & (dist % dilation == 0)
    return np.where(count > 0, np.log2(np.maximum(count, 1)), NEG).astype(np.float32)


def _dil_attn_kernel(bias_ref, q_ref, k_ref, v_ref, o_ref, *, tq):
    seq = q_ref.shape[1]
    for qb in range(seq // tq):
        lo, hi = qb * tq, (qb + 1) * tq
        s = lax.dot_general(q_ref[0, lo:hi, :], k_ref[0, :hi, :], (((1,), (1,)), ((), ())),
                            preferred_element_type=F32)
        s = s + bias_ref[:, seq - hi:]
        m = jnp.max(s, axis=-1, keepdims=True)
        p = jnp.exp2(s - m)
        l = jnp.sum(p, axis=-1, keepdims=True)
        pv = jnp.dot(p.astype(BF16), v_ref[0, :hi, :], preferred_element_type=F32)
        o_ref[0, lo:hi, :] = (pv / l).astype(o_ref.dtype)


def _dilated_attention(proj, *, n_heads, col0_q, col0_k, col0_v, tq=256):
    b, s, _ = proj.shape
    assert s % tq == 0
    assert all(s % d == 0 for _, d in DILATION_CONFIGS)
    bias = jnp.asarray(_dilated_bias_table(s, tq))
    return pl.pallas_call(
        functools.partial(_dil_attn_kernel, tq=tq),
        grid=(b, n_heads),
        in_specs=[pl.BlockSpec(bias.shape, lambda bi, h: (0, 0)),
                  pl.BlockSpec((1, s, HEAD_DIM), lambda bi, h: (bi, 0, col0_q + h)),
                  pl.BlockSpec((1, s, HEAD_DIM), lambda bi, h: (bi, 0, col0_k + h)),
                  pl.BlockSpec((1, s, HEAD_DIM), lambda bi, h: (bi, 0, col0_v + h))],
        out_specs=pl.BlockSpec((1, s, HEAD_DIM), lambda bi, h: (bi, 0, h)),
        out_shape=jax.ShapeDtypeStruct((b, s, n_heads * HEAD_DIM), BF16),
        compiler_params=pltpu.CompilerParams(
            dimension_semantics=("parallel", "parallel"), vmem_limit_bytes=VMEM_LIMIT),
        name="dilated_attn",
    )(bias, proj, proj, proj)


def _layer_norm_rows(y, g, b):
    mu = jnp.mean(y, axis=-1, keepdims=True)
    yc = y - mu
    var = jnp.mean(yc * yc, axis=-1, keepdims=True)
    return yc * lax.rsqrt(var + LN_EPS) * g + b


def _out_ln_kernel(oa_ref, ob_ref, wa_ref, wb_ref, x_ref, g_ref, b_ref, y_ref, *, alpha):
    attn = jnp.dot(oa_ref[...], wa_ref[...], preferred_element_type=F32)
    attn = attn + jnp.dot(ob_ref[...], wb_ref[...], preferred_element_type=F32)
    y = alpha * x_ref[...] + attn
    y_ref[...] = _layer_norm_rows(y, g_ref[...], b_ref[...])


def _out_proj_ln(o_a, o_b, w_out_bf16, x2d, g, b, *, alpha, tm=512):
    m, d = x2d.shape
    ka, kb = o_a.shape[1], o_b.shape[1]
    assert m % tm == 0 and w_out_bf16.shape == (ka + kb, d) and ka == kb
    const = dict(pipeline_mode=pl.Buffered(1))
    return pl.pallas_call(
        functools.partial(_out_ln_kernel, alpha=alpha),
        grid=(m // tm,),
        in_specs=[pl.BlockSpec((tm, ka), lambda i: (i, 0)),
                  pl.BlockSpec((tm, kb), lambda i: (i, 0)),
                  pl.BlockSpec((ka, d), lambda i: (0, 0), **const),
                  pl.BlockSpec((kb, d), lambda i: (1, 0), **const),
                  pl.BlockSpec((tm, d), lambda i: (i, 0)),
                  pl.BlockSpec((1, d), lambda i: (0, 0)),
                  pl.BlockSpec((1, d), lambda i: (0, 0))],
        out_specs=pl.BlockSpec((tm, d), lambda i: (i, 0)),
        out_shape=jax.ShapeDtypeStruct((m, d), F32),
        compiler_params=pltpu.CompilerParams(
            dimension_semantics=("parallel",), vmem_limit_bytes=VMEM_LIMIT),
        name="out_proj_ln",
    )(o_a, o_b, w_out_bf16, w_out_bf16, x2d, g, b)


def _ffn_kernel(x_ref, w1_ref, w2_ref, g_ref, b_ref, o_ref, xb_ref, acc_ref, *, alpha):
    f = pl.program_id(1)

    @pl.when(f == 0)
    def _():
        xb_ref[...] = x_ref[...].astype(BF16)

    h = jnp.dot(xb_ref[...], w1_ref[...], preferred_element_type=F32)
    h = jnp.square(jnp.maximum(h, 0.0)).astype(BF16)
    part = jnp.dot(h, w2_ref[...], preferred_element_type=F32)

    @pl.when(f == 0)
    def _():
        acc_ref[...] = part

    @pl.when(f > 0)
    def _():
        acc_ref[...] += part

    @pl.when(f == pl.num_programs(1) - 1)
    def _():
        y = alpha * x_ref[...] + acc_ref[...]
        o_ref[...] = _layer_norm_rows(y, g_ref[...], b_ref[...])


def _ffn_ln(x2d, w1_bf16, w2_bf16, g, b, *, alpha, tm=512, tf=1024):
    m, d = x2d.shape
    dff = w1_bf16.shape[1]
    assert m % tm == 0 and dff % tf == 0
    return pl.pallas_call(
        functools.partial(_ffn_kernel, alpha=alpha),
        grid=(m // tm, dff // tf),
        in_specs=[pl.BlockSpec((tm, d), lambda i, f: (i, 0)),
                  pl.BlockSpec((d, tf), lambda i, f: (0, f)),
                  pl.BlockSpec((tf, d), lambda i, f: (f, 0)),
                  pl.BlockSpec((1, d), lambda i, f: (0, 0)),
                  pl.BlockSpec((1, d), lambda i, f: (0, 0))],
        out_specs=pl.BlockSpec((tm, d), lambda i, f: (i, 0)),
        out_shape=jax.ShapeDtypeStruct((m, d), F32),
        scratch_shapes=[pltpu.VMEM((tm, d), BF16), pltpu.VMEM((tm, d), F32)],
        compiler_params=pltpu.CompilerParams(
            dimension_semantics=("parallel", "arbitrary"), vmem_limit_bytes=VMEM_LIMIT),
        name="ffn_ln",
    )(x2d, w1_bf16, w2_bf16, g, b)


def kernel(x, w_in, lambda_q1, lambda_k1, lambda_q2, lambda_k2, subln_g, w_out, ln1_g, ln1_b,
           w_ff1, w_ff2, ln2_g, ln2_b):
    bsz, seq, d_model = x.shape
    depth = w_in.shape[0]
    n_heads = d_model // (2 * HEAD_DIM)
    width = n_heads * HEAD_DIM
    alpha = (2.0 * depth) ** 0.25
    blocks_per_group = width // HEAD_DIM

    h2d = x.reshape(bsz * seq, d_model)
    for l in range(depth):
        lambda_init = 0.8 - 0.6 * math.exp(-0.3 * l)
        proj = _projection(h2d, w_in[l].astype(BF16), seq).reshape(bsz, seq, 6 * width)
        lam_vecs = jnp.stack([lambda_q1[l], lambda_k1[l], lambda_q2[l], lambda_k2[l]]).astype(F32)
        o_a = _diff_attention(proj, lam_vecs, subln_g[l].astype(F32).reshape(1, HEAD_DIM),
                              n_heads=n_heads, col0_q=0, col0_k=blocks_per_group,
                              col0_v=2 * blocks_per_group, lambda_init=lambda_init)
        o_b = _dilated_attention(proj, n_heads=n_heads, col0_q=3 * blocks_per_group,
                                 col0_k=4 * blocks_per_group, col0_v=5 * blocks_per_group)
        h2d = _out_proj_ln(o_a.reshape(bsz * seq, width), o_b.reshape(bsz * seq, width),
                           w_out[l].astype(BF16), h2d,
                           ln1_g[l].reshape(1, d_model), ln1_b[l].reshape(1, d_model), alpha=alpha)
        h2d = _ffn_ln(h2d, w_ff1[l].astype(BF16), w_ff2[l].astype(BF16),
                      ln2_g[l].reshape(1, d_model), ln2_b[l].reshape(1, d_model), alpha=alpha)
    return h2d.reshape(bsz, seq, d_model)
```

```python
import functools
import math

import numpy as np
import jax
import jax.numpy as jnp
from jax import lax
from jax.experimental import pallas as pl
from jax.experimental.pallas import tpu as pltpu

F32 = jnp.float32
BF16 = jnp.bfloat16

HEAD_DIM = 128
DIFF_QK_DIM = 64
ROPE_THETA = 10000.0
LN_EPS = 1e-5
SUBLN_EPS = 1e-5
NEG = -1e30
LOG2E = 1.4426950408889634
DILATION_CONFIGS = ((128, 1), (512, 4), (2048, 16))

V7X_LANES = 128
VMEM_LIMIT = 56 * 1024 * 1024


def _proj_kernel(x_ref, w_ref, cosa_ref, sina_ref, cosb_ref, sinb_ref, o_ref,
                 *, group_cols, qa_scale, qb_scale):
    xb = x_ref[...].astype(BF16)
    cos_a, sin_a = cosa_ref[...], sina_ref[...]
    cos_b, sin_b = cosb_ref[...], sinb_ref[...]
    lane = lax.broadcasted_iota(jnp.int32, cos_a.shape, 1)
    first_half = (lane % DIFF_QK_DIM) < (DIFF_QK_DIM // 2)

    def rope_a(t):
        rot = jnp.where(first_half, pltpu.roll(t, 96, 1), pltpu.roll(t, 32, 1))
        return t * cos_a + rot * sin_a

    def rope_b(t):
        return t * cos_b + pltpu.roll(t, 64, 1) * sin_b

    groups = ((rope_a, qa_scale), (rope_a, None), (None, None),
              (rope_b, qb_scale), (rope_b, None), (None, None))
    for gi, (rope, scale) in enumerate(groups):
        c0 = gi * group_cols
        acc = jnp.dot(xb, w_ref[:, c0:c0 + group_cols], preferred_element_type=F32)
        for h in range(group_cols // HEAD_DIM):
            y = acc[:, h * HEAD_DIM:(h + 1) * HEAD_DIM]
            if rope is not None:
                y = rope(y)
            if scale is not None:
                y = y * scale
            o_ref[:, c0 + h * HEAD_DIM:c0 + (h + 1) * HEAD_DIM] = y.astype(o_ref.dtype)


def _rope_tables(seq):
    def ang(dim):
        inv = 1.0 / (ROPE_THETA ** (jnp.arange(0, dim, 2, dtype=F32) / dim))
        return jnp.arange(seq, dtype=F32)[:, None] * inv[None, :]
    a = ang(DIFF_QK_DIM)
    ca, sa = jnp.cos(a), jnp.sin(a)
    cos_a = jnp.concatenate([ca, ca, ca, ca], axis=-1)
    sin_a = jnp.concatenate([-sa, sa, -sa, sa], axis=-1)
    b = ang(HEAD_DIM)
    cb, sb = jnp.cos(b), jnp.sin(b)
    cos_b = jnp.concatenate([cb, cb], axis=-1)
    sin_b = jnp.concatenate([-sb, sb], axis=-1)
    return cos_a, sin_a, cos_b, sin_b


def _projection(x2d, w_bf16, seq, *, tm=512):
    m, k = x2d.shape
    n = w_bf16.shape[1]
    assert m % tm == 0 and seq % tm == 0 and n % (6 * HEAD_DIM) == 0
    cos_a, sin_a, cos_b, sin_b = _rope_tables(seq)
    pos_blocks = seq // tm
    tab_spec = pl.BlockSpec((tm, HEAD_DIM), lambda i: (i % pos_blocks, 0))
    kern = functools.partial(
        _proj_kernel, group_cols=n // 6,
        qa_scale=DIFF_QK_DIM ** -0.5 * LOG2E, qb_scale=HEAD_DIM ** -0.5 * LOG2E)
    return pl.pallas_call(
        kern,
        grid=(m // tm,),
        in_specs=[pl.BlockSpec((tm, k), lambda i: (i, 0)),
                  pl.BlockSpec((k, n), lambda i: (0, 0), pipeline_mode=pl.Buffered(1)),
                  tab_spec, tab_spec, tab_spec, tab_spec],
        out_specs=pl.BlockSpec((tm, n), lambda i: (i, 0)),
        out_shape=jax.ShapeDtypeStruct((m, n), BF16),
        compiler_params=pltpu.CompilerParams(
            dimension_semantics=("parallel",), vmem_limit_bytes=VMEM_LIMIT),
        name="proj_rope",
    )(x2d, w_bf16, cos_a, sin_a, cos_b, sin_b)


def _diff_attn_kernel(lam_ref, g_ref, q_ref, k_ref, v_ref, o_ref, *, tq, lambda_init):
    seq = q_ref.shape[1]
    lv = lam_ref[...]
    lam = (jnp.exp(jnp.sum(lv[0:1] * lv[1:2], axis=-1, keepdims=True))
           - jnp.exp(jnp.sum(lv[2:3] * lv[3:4], axis=-1, keepdims=True)) + lambda_init)
    g = g_ref[...]

    lane = lax.broadcasted_iota(jnp.int32, (tq, HEAD_DIM), 1)
    row = lax.broadcasted_iota(jnp.int32, (2 * tq, tq), 0)
    col = lax.broadcasted_iota(jnp.int32, (2 * tq, tq), 1)
    visible = col <= jnp.where(row >= tq, row - tq, row)

    for qb in range(seq // tq):
        lo, hi = qb * tq, (qb + 1) * tq
        q = q_ref[0, lo:hi, :]
        zero = jnp.zeros_like(q)
        q2 = jnp.concatenate([jnp.where(lane < DIFF_QK_DIM, q, zero),
                              jnp.where(lane >= DIFF_QK_DIM, q, zero)], axis=0)
        s = lax.dot_general(q2, k_ref[0, :hi, :], (((1,), (1,)), ((), ())),
                            preferred_element_type=F32)
        s_diag = jnp.where(visible, s[:, lo:], NEG)
        m = jnp.max(s_diag, axis=-1, keepdims=True)
        if qb > 0:
            m = jnp.maximum(m, jnp.max(s[:, :lo], axis=-1, keepdims=True))
        p_diag = jnp.exp2(s_diag - m)
        l = jnp.sum(p_diag, axis=-1, keepdims=True)
        pv = jnp.dot(p_diag.astype(BF16), v_ref[0, lo:hi, :], preferred_element_type=F32)
        if qb > 0:
            p_main = jnp.exp2(s[:, :lo] - m)
            l = l + jnp.sum(p_main, axis=-1, keepdims=True)
            pv = pv + jnp.dot(p_main.astype(BF16), v_ref[0, :lo, :], preferred_element_type=F32)
        o = pv / l
        d = o[:tq] - lam * o[tq:]
        d = d * lax.rsqrt(jnp.mean(d * d, axis=-1, keepdims=True) + SUBLN_EPS) * g
        o_ref[0, lo:hi, :] = (d * (1.0 - lambda_init)).astype(o_ref.dtype)


def _diff_attention(proj, lam_vecs, subln_g, *, n_heads, col0_q, col0_k, col0_v, lambda_init, tq=256):
    b, s, _ = proj.shape
    assert s % tq == 0
    kern = functools.partial(_diff_attn_kernel, tq=tq, lambda_init=lambda_init)
    return pl.pallas_call(
        kern,
        grid=(b, n_heads),
        in_specs=[pl.BlockSpec(lam_vecs.shape, lambda bi, h: (0, 0)),
                  pl.BlockSpec((1, HEAD_DIM), lambda bi, h: (0, 0)),
                  pl.BlockSpec((1, s, HEAD_DIM), lambda bi, h: (bi, 0, col0_q + h)),
                  pl.BlockSpec((1, s, HEAD_DIM), lambda bi, h: (bi, 0, col0_k + h)),
                  pl.BlockSpec((1, s, HEAD_DIM), lambda bi, h: (bi, 0, col0_v + h))],
        out_specs=pl.BlockSpec((1, s, HEAD_DIM), lambda bi, h: (bi, 0, h)),
        out_shape=jax.ShapeDtypeStruct((b, s, n_heads * HEAD_DIM), BF16),
        compiler_params=pltpu.CompilerParams(
            dimension_semantics=("parallel", "parallel"), vmem_limit_bytes=VMEM_LIMIT),
        name="diff_attn",
    )(lam_vecs, subln_g, proj, proj, proj)


def _dilated_bias_table(seq, tq):
    dist = np.arange(tq)[:, None] + (seq - tq) - np.arange(seq)[None, :]
    count = np.zeros(dist.shape, np.int64)
    for window, dilation in DILATION_CONFIGS:
        count += (dist >= 0) & (dist <= window) & (dist % dilation == 0)
    return np.where(count > 0, np.log2(np.maximum(count, 1)), NEG).astype(np.float32)


def _dil_attn_kernel(bias_ref, q_ref, k_ref, v_ref, o_ref, *, tq):
    seq = q_ref.shape[1]
    for qb in range(seq // tq):
        lo, hi = qb * tq, (qb + 1) * tq
        s = lax.dot_general(q_ref[0, lo:hi, :], k_ref[0, :hi, :], (((1,), (1,)), ((), ())),
                            preferred_element_type=F32)
        s = s + bias_ref[:, seq - hi:]
        m = jnp.max(s, axis=-1, keepdims=True)
        p = jnp.exp2(s - m)
        l = jnp.sum(p, axis=-1, keepdims=True)
        pv = jnp.dot(p.astype(BF16), v_ref[0, :hi, :], preferred_element_type=F32)
        o_ref[0, lo:hi, :] = (pv / l).astype(o_ref.dtype)


def _dilated_attention(proj, *, n_heads, col0_q, col0_k, col0_v, tq=256):
    b, s, _ = proj.shape
    assert s % tq == 0
    assert all(s % d == 0 for _, d in DILATION_CONFIGS)
    bias = jnp.asarray(_dilated_bias_table(s, tq))
    return pl.pallas_call(
        functools.partial(_dil_attn_kernel, tq=tq),
        grid=(b, n_heads),
        in_specs=[pl.BlockSpec(bias.shape, lambda bi, h: (0, 0)),
                  pl.BlockSpec((1, s, HEAD_DIM), lambda bi, h: (bi, 0, col0_q + h)),
                  pl.BlockSpec((1, s, HEAD_DIM), lambda bi, h: (bi, 0, col0_k + h)),
                  pl.BlockSpec((1, s, HEAD_DIM), lambda bi, h: (bi, 0, col0_v + h))],
        out_specs=pl.BlockSpec((1, s, HEAD_DIM), lambda bi, h: (bi, 0, h)),
        out_shape=jax.ShapeDtypeStruct((b, s, n_heads * HEAD_DIM), BF16),
        compiler_params=pltpu.CompilerParams(
            dimension_semantics=("parallel", "parallel"), vmem_limit_bytes=VMEM_LIMIT),
        name="dilated_attn",
    )(bias, proj, proj, proj)


def _layer_norm_rows(y, g, b):
    mu = jnp.mean(y, axis=-1, keepdims=True)
    yc = y - mu
    var = jnp.mean(yc * yc, axis=-1, keepdims=True)
    return yc * lax.rsqrt(var + LN_EPS) * g + b


def _out_ln_kernel(oa_ref, ob_ref, wa_ref, wb_ref, x_ref, g_ref, b_ref, y_ref, *, alpha):
    attn = jnp.dot(oa_ref[...], wa_ref[...], preferred_element_type=F32)
    attn = attn + jnp.dot(ob_ref[...], wb_ref[...], preferred_element_type=F32)
    y = alpha * x_ref[...] + attn
    y_ref[...] = _layer_norm_rows(y, g_ref[...], b_ref[...])


def _out_proj_ln(o_a, o_b, w_out_bf16, x2d, g, b, *, alpha, tm=512):
    m, d = x2d.shape
    ka, kb = o_a.shape[1], o_b.shape[1]
    assert m % tm == 0 and w_out_bf16.shape == (ka + kb, d) and ka == kb
    const = dict(pipeline_mode=pl.Buffered(1))
    return pl.pallas_call(
        functools.partial(_out_ln_kernel, alpha=alpha),
        grid=(m // tm,),
        in_specs=[pl.BlockSpec((tm, ka), lambda i: (i, 0)),
                  pl.BlockSpec((tm, kb), lambda i: (i, 0)),
                  pl.BlockSpec((ka, d), lambda i: (0, 0), **const),
                  pl.BlockSpec((kb, d), lambda i: (1, 0), **const),
                  pl.BlockSpec((tm, d), lambda i: (i, 0)),
                  pl.BlockSpec((1, d), lambda i: (0, 0)),
                  pl.BlockSpec((1, d), lambda i: (0, 0))],
        out_specs=pl.BlockSpec((tm, d), lambda i: (i, 0)),
        out_shape=jax.ShapeDtypeStruct((m, d), F32),
        compiler_params=pltpu.CompilerParams(
            dimension_semantics=("parallel",), vmem_limit_bytes=VMEM_LIMIT),
        name="out_proj_ln",
    )(o_a, o_b, w_out_bf16, w_out_bf16, x2d, g, b)


def _ffn_kernel(x_ref, w1_ref, w2_ref, g_ref, b_ref, o_ref, xb_ref, acc_ref, *, alpha):
    f = pl.program_id(1)

    @pl.when(f == 0)
    def _():
        xb_ref[...] = x_ref[...].astype(BF16)
        acc_ref[...] = jnp.zeros_like(acc_ref)

    h = jnp.dot(xb_ref[...], w1_ref[...], preferred_element_type=F32)
    h = jnp.square(jnp.maximum(h, 0.0)).astype(BF16)
    acc_ref[...] += jnp.dot(h, w2_ref[...], preferred_element_type=F32)

    @pl.when(f == pl.num_programs(1) - 1)
    def _():
        y = alpha * x_ref[...] + acc_ref[...]
        o_ref[...] = _layer_norm_rows(y, g_ref[...], b_ref[...])


def _ffn_ln(x2d, w1_bf16, w2_bf16, g, b, *, alpha, tm=512, tf=1024):
    m, d = x2d.shape
    dff = w1_bf16.shape[1]
    assert m % tm == 0 and dff % tf == 0
    return pl.pallas_call(
        functools.partial(_ffn_kernel, alpha=alpha),
        grid=(m // tm, dff // tf),
        in_specs=[pl.BlockSpec((tm, d), lambda i, f: (i, 0)),
                  pl.BlockSpec((d, tf), lambda i, f: (0, f)),
                  pl.BlockSpec((tf, d), lambda i, f: (f, 0)),
                  pl.BlockSpec((1, d), lambda i, f: (0, 0)),
                  pl.BlockSpec((1, d), lambda i, f: (0, 0))],
        out_specs=pl.BlockSpec((tm, d), lambda i, f: (i, 0)),
        out_shape=jax.ShapeDtypeStruct((m, d), F32),
        scratch_shapes=[pltpu.VMEM((tm, d), BF16), pltpu.VMEM((tm, d), F32)],
        compiler_params=pltpu.CompilerParams(
            dimension_semantics=("parallel", "arbitrary"), vmem_limit_bytes=VMEM_LIMIT),
        name="ffn_ln",
    )(x2d, w1_bf16, w2_bf16, g, b)


def kernel(x, w_in, lambda_q1, lambda_k1, lambda_q2, lambda_k2, subln_g, w_out, ln1_g, ln1_b,
           w_ff1, w_ff2, ln2_g, ln2_b):
    bsz, seq, d_model = x.shape
    depth = w_in.shape[0]
    n_heads = d_model // (2 * HEAD_DIM)
    width = n_heads * HEAD_DIM
    alpha = (2.0 * depth) ** 0.25
    blocks_per_group = width // HEAD_DIM

    h2d = x.reshape(bsz * seq, d_model)
    for l in range(depth):
        lambda_init = 0.8 - 0.6 * math.exp(-0.3 * l)
        proj = _projection(h2d, w_in[l].astype(BF16), seq).reshape(bsz, seq, 6 * width)
        lam_vecs = jnp.stack([lambda_q1[l], lambda_k1[l], lambda_q2[l], lambda_k2[l]]).astype(F32)
        o_a = _diff_attention(proj, lam_vecs, subln_g[l].astype(F32).reshape(1, HEAD_DIM),
                              n_heads=n_heads, col0_q=0, col0_k=blocks_per_group,
                              col0_v=2 * blocks_per_group, lambda_init=lambda_init)
        o_b = _dilated_attention(proj, n_heads=n_heads, col0_q=3 * blocks_per_group,
                                 col0_k=4 * blocks_per_group, col0_v=5 * blocks_per_group)
        h2d = _out_proj_ln(o_a.reshape(bsz * seq, width), o_b.reshape(bsz * seq, width),
                           w_out[l].astype(BF16), h2d,
                           ln1_g[l].reshape(1, d_model), ln1_b[l].reshape(1, d_model), alpha=alpha)
        h2d = _ffn_ln(h2d, w_ff1[l].astype(BF16), w_ff2[l].astype(BF16),
                      ln2_g[l].reshape(1, d_model), ln2_b[l].reshape(1, d_model), alpha=alpha)
    return h2d.reshape(bsz, seq, d_model)
```

```python
import functools
import math

import numpy as np
import jax
import jax.numpy as jnp
from jax import lax
from jax.experimental import pallas as pl
from jax.experimental.pallas import tpu as pltpu

F32 = jnp.float32
BF16 = jnp.bfloat16

HEAD_DIM = 128
DIFF_QK_DIM = 64
ROPE_THETA = 10000.0
LN_EPS = 1e-5
SUBLN_EPS = 1e-5
NEG = -1e30
LOG2E = 1.4426950408889634
DILATION_CONFIGS = ((128, 1), (512, 4), (2048, 16))

V7X_LANES = 128
VMEM_LIMIT = 56 * 1024 * 1024


def _proj_kernel(x_ref, w_ref, cosa_ref, sina_ref, cosb_ref, sinb_ref, o_ref,
                 *, group_cols, qa_scale, qb_scale):
    xb = x_ref[...].astype(BF16)
    cos_a, sin_a = cosa_ref[...], sina_ref[...]
    cos_b, sin_b = cosb_ref[...], sinb_ref[...]
    lane = lax.broadcasted_iota(jnp.int32, cos_a.shape, 1)
    first_half = (lane % DIFF_QK_DIM) < (DIFF_QK_DIM // 2)

    def rope_a(t):
        rot = jnp.where(first_half, pltpu.roll(t, 96, 1), pltpu.roll(t, 32, 1))
        return t * cos_a + rot * sin_a

    def rope_b(t):
        return t * cos_b + pltpu.roll(t, 64, 1) * sin_b

    groups = ((rope_a, qa_scale), (rope_a, None), (None, None),
              (rope_b, qb_scale), (rope_b, None), (None, None))
    for gi, (rope, scale) in enumerate(groups):
        c0 = gi * group_cols
        acc = jnp.dot(xb, w_ref[:, c0:c0 + group_cols], preferred_element_type=F32)
        for h in range(group_cols // HEAD_DIM):
            y = acc[:, h * HEAD_DIM:(h + 1) * HEAD_DIM]
            if rope is not None:
                y = rope(y)
            if scale is not None:
                y = y * scale
            o_ref[:, c0 + h * HEAD_DIM:c0 + (h + 1) * HEAD_DIM] = y.astype(o_ref.dtype)


def _rope_tables(seq):
    def ang(dim):
        inv = 1.0 / (ROPE_THETA ** (jnp.arange(0, dim, 2, dtype=F32) / dim))
        return jnp.arange(seq, dtype=F32)[:, None] * inv[None, :]
    a = ang(DIFF_QK_DIM)
    ca, sa = jnp.cos(a), jnp.sin(a)
    cos_a = jnp.concatenate([ca, ca, ca, ca], axis=-1)
    sin_a = jnp.concatenate([-sa, sa, -sa, sa], axis=-1)
    b = ang(HEAD_DIM)
    cb, sb = jnp.cos(b), jnp.sin(b)
    cos_b = jnp.concatenate([cb, cb], axis=-1)
    sin_b = jnp.concatenate([-sb, sb], axis=-1)
    return cos_a, sin_a, cos_b, sin_b


def _projection(x2d, w_bf16, seq, *, tm=512):
    m, k = x2d.shape
    n = w_bf16.shape[1]
    assert m % tm == 0 and seq % tm == 0 and n % (6 * HEAD_DIM) == 0
    cos_a, sin_a, cos_b, sin_b = _rope_tables(seq)
    pos_blocks = seq // tm
    tab_spec = pl.BlockSpec((tm, HEAD_DIM), lambda i: (i % pos_blocks, 0))
    kern = functools.partial(
        _proj_kernel, group_cols=n // 6,
        qa_scale=DIFF_QK_DIM ** -0.5 * LOG2E, qb_scale=HEAD_DIM ** -0.5 * LOG2E)
    return pl.pallas_call(
        kern,
        grid=(m // tm,),
        in_specs=[pl.BlockSpec((tm, k), lambda i: (i, 0)),
                  pl.BlockSpec((k, n), lambda i: (0, 0), pipeline_mode=pl.Buffered(1)),
                  tab_spec, tab_spec, tab_spec, tab_spec],
        out_specs=pl.BlockSpec((tm, n), lambda i: (i, 0)),
        out_shape=jax.ShapeDtypeStruct((m, n), BF16),
        compiler_params=pltpu.CompilerParams(
            dimension_semantics=("parallel",), vmem_limit_bytes=VMEM_LIMIT),
        name="proj_rope",
    )(x2d, w_bf16, cos_a, sin_a, cos_b, sin_b)


def _with_ones_columns(v):
    return jnp.concatenate([v, jnp.ones_like(v)], axis=1)


def _diff_attn_kernel(lam_ref, g_ref, q_ref, k_ref, v_ref, o_ref, *, tq, heads, lambda_init):
    seq = q_ref.shape[1]
    lv = lam_ref[...]
    lam = (jnp.exp(jnp.sum(lv[0:1] * lv[1:2], axis=-1, keepdims=True))
           - jnp.exp(jnp.sum(lv[2:3] * lv[3:4], axis=-1, keepdims=True)) + lambda_init)
    g = g_ref[...]

    lane = lax.broadcasted_iota(jnp.int32, (tq, HEAD_DIM), 1)
    row = lax.broadcasted_iota(jnp.int32, (2 * tq, tq), 0)
    col = lax.broadcasted_iota(jnp.int32, (2 * tq, tq), 1)
    visible = col <= jnp.where(row >= tq, row - tq, row)

    for h in range(heads):
        hs = slice(h * HEAD_DIM, (h + 1) * HEAD_DIM)
        v_ext = _with_ones_columns(v_ref[0, :, hs])
        for qb in reversed(range(seq // tq)):
            lo, hi = qb * tq, (qb + 1) * tq
            q = q_ref[0, lo:hi, hs]
            zero = jnp.zeros_like(q)
            q2 = jnp.concatenate([jnp.where(lane < DIFF_QK_DIM, q, zero),
                                  jnp.where(lane >= DIFF_QK_DIM, q, zero)], axis=0)
            s = lax.dot_general(q2, k_ref[0, :hi, hs], (((1,), (1,)), ((), ())),
                                preferred_element_type=F32)
            s_diag = jnp.where(visible, s[:, lo:], NEG)
            m = jnp.max(s_diag, axis=-1, keepdims=True)
            if qb > 0:
                m = jnp.maximum(m, jnp.max(s[:, :lo], axis=-1, keepdims=True))
            p = jnp.exp2(s_diag - m)
            if qb > 0:
                p = jnp.concatenate([jnp.exp2(s[:, :lo] - m), p], axis=1)
            r = jnp.dot(p.astype(BF16), v_ext[:hi], preferred_element_type=F32)
            o = r[:, :HEAD_DIM] / r[:, HEAD_DIM:]
            d = o[:tq] - lam * o[tq:]
            d = d * lax.rsqrt(jnp.mean(d * d, axis=-1, keepdims=True) + SUBLN_EPS) * g
            o_ref[0, lo:hi, hs] = (d * (1.0 - lambda_init)).astype(o_ref.dtype)


def _diff_attention(proj, lam_vecs, subln_g, *, n_heads, col0_q, col0_k, col0_v, lambda_init,
                    tq=256, heads=2):
    b, s, _ = proj.shape
    assert s % tq == 0 and n_heads % heads == 0
    assert col0_q % heads == 0 and col0_k % heads == 0 and col0_v % heads == 0
    kern = functools.partial(_diff_attn_kernel, tq=tq, heads=heads, lambda_init=lambda_init)
    w = heads * HEAD_DIM
    return pl.pallas_call(
        kern,
        grid=(b, n_heads // heads),
        in_specs=[pl.BlockSpec(lam_vecs.shape, lambda bi, h: (0, 0)),
                  pl.BlockSpec((1, HEAD_DIM), lambda bi, h: (0, 0)),
                  pl.BlockSpec((1, s, w), lambda bi, h: (bi, 0, col0_q // heads + h)),
                  pl.BlockSpec((1, s, w), lambda bi, h: (bi, 0, col0_k // heads + h)),
                  pl.BlockSpec((1, s, w), lambda bi, h: (bi, 0, col0_v // heads + h))],
        out_specs=pl.BlockSpec((1, s, w), lambda bi, h: (bi, 0, h)),
        out_shape=jax.ShapeDtypeStruct((b, s, n_heads * HEAD_DIM), BF16),
        compiler_params=pltpu.CompilerParams(
            dimension_semantics=("parallel", "parallel"), vmem_limit_bytes=VMEM_LIMIT),
        name="diff_attn",
    )(lam_vecs, subln_g, proj, proj, proj)


def _dilated_bias_table(seq, tq):
    dist = np.arange(tq)[:, None] + (seq - tq) - np.arange(seq)[None, :]
    count = np.zeros(dist.shape, np.int64)
    for window, dilation in DILATION_CONFIGS:
        count += (dist >= 0) & (dist <= window) & (dist % dilation == 0)
    return np.where(count > 0, np.log2(np.maximum(count, 1)), NEG).astype(np.float32)


def _dil_attn_kernel(bias_ref, q_ref, k_ref, v_ref, o_ref, *, tq, heads):
    seq = q_ref.shape[1]
    for h in range(heads):
        hs = slice(h * HEAD_DIM, (h + 1) * HEAD_DIM)
        v_ext = _with_ones_columns(v_ref[0, :, hs])
        for qb in reversed(range(seq // tq)):
            lo, hi = qb * tq, (qb + 1) * tq
            s = lax.dot_general(q_ref[0, lo:hi, hs], k_ref[0, :hi, hs], (((1,), (1,)), ((), ())),
                                preferred_element_type=F32)
            s = s + bias_ref[:, seq - hi:]
            p = jnp.exp2(s - jnp.max(s, axis=-1, keepdims=True))
            r = jnp.dot(p.astype(BF16), v_ext[:hi], preferred_element_type=F32)
            o_ref[0, lo:hi, hs] = (r[:, :HEAD_DIM] / r[:, HEAD_DIM:]).astype(o_ref.dtype)


def _dilated_attention(proj, *, n_heads, col0_q, col0_k, col0_v, tq=256, heads=2):
    b, s, _ = proj.shape
    assert s % tq == 0 and n_heads % heads == 0
    assert col0_q % heads == 0 and col0_k % heads == 0 and col0_v % heads == 0
    assert all(s % d == 0 for _, d in DILATION_CONFIGS)
    bias = jnp.asarray(_dilated_bias_table(s, tq))
    w = heads * HEAD_DIM
    return pl.pallas_call(
        functools.partial(_dil_attn_kernel, tq=tq, heads=heads),
        grid=(b, n_heads // heads),
        in_specs=[pl.BlockSpec(bias.shape, lambda bi, h: (0, 0)),
                  pl.BlockSpec((1, s, w), lambda bi, h: (bi, 0, col0_q // heads + h)),
                  pl.BlockSpec((1, s, w), lambda bi, h: (bi, 0, col0_k // heads + h)),
                  pl.BlockSpec((1, s, w), lambda bi, h: (bi, 0, col0_v // heads + h))],
        out_specs=pl.BlockSpec((1, s, w), lambda bi, h: (bi, 0, h)),
        out_shape=jax.ShapeDtypeStruct((b, s, n_heads * HEAD_DIM), BF16),
        compiler_params=pltpu.CompilerParams(
            dimension_semantics=("parallel", "parallel"), vmem_limit_bytes=VMEM_LIMIT),
        name="dilated_attn",
    )(bias, proj, proj, proj)


def _layer_norm_rows(y, g, b):
    mu = jnp.mean(y, axis=-1, keepdims=True)
    yc = y - mu
    var = jnp.mean(yc * yc, axis=-1, keepdims=True)
    return yc * lax.rsqrt(var + LN_EPS) * g + b


def _out_ln_kernel(oa_ref, ob_ref, wa_ref, wb_ref, x_ref, g_ref, b_ref, y_ref, *, alpha):
    attn = jnp.dot(oa_ref[...], wa_ref[...], preferred_element_type=F32)
    attn = attn + jnp.dot(ob_ref[...], wb_ref[...], preferred_element_type=F32)
    y = alpha * x_ref[...] + attn
    y_ref[...] = _layer_norm_rows(y, g_ref[...], b_ref[...])


def _out_proj_ln(o_a, o_b, w_out_bf16, x2d, g, b, *, alpha, tm=512):
    m, d = x2d.shape
    ka, kb = o_a.shape[1], o_b.shape[1]
    assert m % tm == 0 and w_out_bf16.shape == (ka + kb, d) and ka == kb
    const = dict(pipeline_mode=pl.Buffered(1))
    return pl.pallas_call(
        functools.partial(_out_ln_kernel, alpha=alpha),
        grid=(m // tm,),
        in_specs=[pl.BlockSpec((tm, ka), lambda i: (i, 0)),
                  pl.BlockSpec((tm, kb), lambda i: (i, 0)),
                  pl.BlockSpec((ka, d), lambda i: (0, 0), **const),
                  pl.BlockSpec((kb, d), lambda i: (1, 0), **const),
                  pl.BlockSpec((tm, d), lambda i: (i, 0)),
                  pl.BlockSpec((1, d), lambda i: (0, 0)),
                  pl.BlockSpec((1, d), lambda i: (0, 0))],
        out_specs=pl.BlockSpec((tm, d), lambda i: (i, 0)),
        out_shape=jax.ShapeDtypeStruct((m, d), F32),
        compiler_params=pltpu.CompilerParams(
            dimension_semantics=("parallel",), vmem_limit_bytes=VMEM_LIMIT),
        name="out_proj_ln",
    )(o_a, o_b, w_out_bf16, w_out_bf16, x2d, g, b)


def _ffn_kernel(x_ref, w1_ref, w2_ref, g_ref, b_ref, o_ref, xb_ref, acc_ref, *, alpha):
    f = pl.program_id(1)

    @pl.when(f == 0)
    def _():
        xb_ref[...] = x_ref[...].astype(BF16)
        acc_ref[...] = jnp.zeros_like(acc_ref)

    h = jnp.dot(xb_ref[...], w1_ref[...], preferred_element_type=F32)
    h = jnp.square(jnp.maximum(h, 0.0)).astype(BF16)
    acc_ref[...] += jnp.dot(h, w2_ref[...], preferred_element_type=F32)

    @pl.when(f == pl.num_programs(1) - 1)
    def _():
        y = alpha * x_ref[...] + acc_ref[...]
        o_ref[...] = _layer_norm_rows(y, g_ref[...], b_ref[...])


def _ffn_ln(x2d, w1_bf16, w2_bf16, g, b, *, alpha, tm=512, tf=1024):
    m, d = x2d.shape
    dff = w1_bf16.shape[1]
    assert m % tm == 0 and dff % tf == 0
    return pl.pallas_call(
        functools.partial(_ffn_kernel, alpha=alpha),
        grid=(m // tm, dff // tf),
        in_specs=[pl.BlockSpec((tm, d), lambda i, f: (i, 0)),
                  pl.BlockSpec((d, tf), lambda i, f: (0, f)),
                  pl.BlockSpec((tf, d), lambda i, f: (f, 0)),
                  pl.BlockSpec((1, d), lambda i, f: (0, 0)),
                  pl.BlockSpec((1, d), lambda i, f: (0, 0))],
        out_specs=pl.BlockSpec((tm, d), lambda i, f: (i, 0)),
        out_shape=jax.ShapeDtypeStruct((m, d), F32),
        scratch_shapes=[pltpu.VMEM((tm, d), BF16), pltpu.VMEM((tm, d), F32)],
        compiler_params=pltpu.CompilerParams(
            dimension_semantics=("parallel", "arbitrary"), vmem_limit_bytes=VMEM_LIMIT),
        name="ffn_ln",
    )(x2d, w1_bf16, w2_bf16, g, b)


def kernel(x, w_in, lambda_q1, lambda_k1, lambda_q2, lambda_k2, subln_g, w_out, ln1_g, ln1_b,
           w_ff1, w_ff2, ln2_g, ln2_b):
    bsz, seq, d_model = x.shape
    depth = w_in.shape[0]
    n_heads = d_model // (2 * HEAD_DIM)
    width = n_heads * HEAD_DIM
    alpha = (2.0 * depth) ** 0.25
    blocks_per_group = width // HEAD_DIM

    h2d = x.reshape(bsz * seq, d_model)
    for l in range(depth):
        lambda_init = 0.8 - 0.6 * math.exp(-0.3 * l)
        proj = _projection(h2d, w_in[l].astype(BF16), seq).reshape(bsz, seq, 6 * width)
        lam_vecs = jnp.stack([lambda_q1[l], lambda_k1[l], lambda_q2[l], lambda_k2[l]]).astype(F32)
        o_a = _diff_attention(proj, lam_vecs, subln_g[l].astype(F32).reshape(1, HEAD_DIM),
                              n_heads=n_heads, col0_q=0, col0_k=blocks_per_group,
                              col0_v=2 * blocks_per_group, lambda_init=lambda_init)
        o_b = _dilated_attention(proj, n_heads=n_heads, col0_q=3 * blocks_per_group,
                                 col0_k=4 * blocks_per_group, col0_v=5 * blocks_per_group)
        h2d = _out_proj_ln(o_a.reshape(bsz * seq, width), o_b.reshape(bsz * seq, width),
                           w_out[l].astype(BF16), h2d,
                           ln1_g[l].reshape(1, d_model), ln1_b[l].reshape(1, d_model), alpha=alpha)
        h2d = _ffn_ln(h2d, w_ff1[l].astype(BF16), w_ff2[l].astype(BF16),
                      ln2_g[l].reshape(1, d_model), ln2_b[l].reshape(1, d_model), alpha=alpha)
    return h2d.reshape(bsz, seq, d_model)
```

```python
import functools
import math

import numpy as np
import jax
import jax.numpy as jnp
from jax import lax
from jax.experimental import pallas as pl
from jax.experimental.pallas import tpu as pltpu

F32 = jnp.float32
BF16 = jnp.bfloat16

HEAD_DIM = 128
DIFF_QK_DIM = 64
ROPE_THETA = 10000.0
LN_EPS = 1e-5
SUBLN_EPS = 1e-5
NEG = -1e30
LOG2E = 1.4426950408889634
DILATION_CONFIGS = ((128, 1), (512, 4), (2048, 16))

V7X_LANES = 128
VMEM_LIMIT = 56 * 1024 * 1024


def _proj_kernel(x_ref, w_ref, cosa_ref, sina_ref, cosb_ref, sinb_ref, o_ref,
                 *, group_cols, qa_scale, qb_scale):
    xb = x_ref[...].astype(BF16)
    cos_a, sin_a = cosa_ref[...], sina_ref[...]
    cos_b, sin_b = cosb_ref[...], sinb_ref[...]
    lane = lax.broadcasted_iota(jnp.int32, cos_a.shape, 1)
    first_half = (lane % DIFF_QK_DIM) < (DIFF_QK_DIM // 2)

    def rope_a(t):
        rot = jnp.where(first_half, pltpu.roll(t, 96, 1), pltpu.roll(t, 32, 1))
        return t * cos_a + rot * sin_a

    def rope_b(t):
        return t * cos_b + pltpu.roll(t, 64, 1) * sin_b

    groups = ((rope_a, qa_scale), (rope_a, None), (None, None),
              (rope_b, qb_scale), (rope_b, None), (None, None))
    for gi, (rope, scale) in enumerate(groups):
        c0 = gi * group_cols
        acc = jnp.dot(xb, w_ref[:, c0:c0 + group_cols], preferred_element_type=F32)
        for h in range(group_cols // HEAD_DIM):
            y = acc[:, h * HEAD_DIM:(h + 1) * HEAD_DIM]
            if rope is not None:
                y = rope(y)
            if scale is not None:
                y = y * scale
            o_ref[:, c0 + h * HEAD_DIM:c0 + (h + 1) * HEAD_DIM] = y.astype(o_ref.dtype)


def _rope_tables(seq):
    def ang(dim):
        inv = 1.0 / (ROPE_THETA ** (jnp.arange(0, dim, 2, dtype=F32) / dim))
        return jnp.arange(seq, dtype=F32)[:, None] * inv[None, :]
    a = ang(DIFF_QK_DIM)
    ca, sa = jnp.cos(a), jnp.sin(a)
    cos_a = jnp.concatenate([ca, ca, ca, ca], axis=-1)
    sin_a = jnp.concatenate([-sa, sa, -sa, sa], axis=-1)
    b = ang(HEAD_DIM)
    cb, sb = jnp.cos(b), jnp.sin(b)
    cos_b = jnp.concatenate([cb, cb], axis=-1)
    sin_b = jnp.concatenate([-sb, sb], axis=-1)
    return cos_a, sin_a, cos_b, sin_b


def _projection(x2d, w_bf16, seq, *, tm=512):
    m, k = x2d.shape
    n = w_bf16.shape[1]
    assert m % tm == 0 and seq % tm == 0 and n % (6 * HEAD_DIM) == 0
    cos_a, sin_a, cos_b, sin_b = _rope_tables(seq)
    pos_blocks = seq // tm
    tab_spec = pl.BlockSpec((tm, HEAD_DIM), lambda i: (i % pos_blocks, 0))
    kern = functools.partial(
        _proj_kernel, group_cols=n // 6,
        qa_scale=DIFF_QK_DIM ** -0.5 * LOG2E, qb_scale=HEAD_DIM ** -0.5 * LOG2E)
    return pl.pallas_call(
        kern,
        grid=(m // tm,),
        in_specs=[pl.BlockSpec((tm, k), lambda i: (i, 0)),
                  pl.BlockSpec((k, n), lambda i: (0, 0), pipeline_mode=pl.Buffered(1)),
                  tab_spec, tab_spec, tab_spec, tab_spec],
        out_specs=pl.BlockSpec((tm, n), lambda i: (i, 0)),
        out_shape=jax.ShapeDtypeStruct((m, n), BF16),
        compiler_params=pltpu.CompilerParams(
            dimension_semantics=("parallel",), vmem_limit_bytes=VMEM_LIMIT),
        name="proj_rope",
    )(x2d, w_bf16, cos_a, sin_a, cos_b, sin_b)


def _with_ones_columns(v):
    return jnp.concatenate([v, jnp.ones_like(v)], axis=1)


def _ride_along_specs(weights, n_steps, step_index):
    in_specs, out_specs, out_shapes = [], [], []
    for w in weights:
        rows, cols = w.shape
        assert rows % n_steps == 0 and (rows // n_steps) % 16 == 0
        spec = pl.BlockSpec((rows // n_steps, cols), lambda *idx: (step_index(*idx), 0))
        in_specs.append(spec)
        out_specs.append(spec)
        out_shapes.append(jax.ShapeDtypeStruct(w.shape, BF16))
    return in_specs, out_specs, out_shapes


def _cast_ride_along(src_refs, dst_refs):
    for src, dst in zip(src_refs, dst_refs):
        dst[...] = src[...].astype(dst.dtype)


def _diff_attn_kernel(lam_ref, g_ref, q_ref, k_ref, v_ref, *rest, tq, heads, lambda_init):
    n_ride = (len(rest) - 1) // 2
    o_ref = rest[n_ride]
    _cast_ride_along(rest[:n_ride], rest[n_ride + 1:])
    seq = q_ref.shape[1]
    lv = lam_ref[...]
    lam = (jnp.exp(jnp.sum(lv[0:1] * lv[1:2], axis=-1, keepdims=True))
           - jnp.exp(jnp.sum(lv[2:3] * lv[3:4], axis=-1, keepdims=True)) + lambda_init)
    g = g_ref[...]

    lane = lax.broadcasted_iota(jnp.int32, (tq, HEAD_DIM), 1)
    row = lax.broadcasted_iota(jnp.int32, (2 * tq, tq), 0)
    col = lax.broadcasted_iota(jnp.int32, (2 * tq, tq), 1)
    visible = col <= jnp.where(row >= tq, row - tq, row)

    for h in range(heads):
        hs = slice(h * HEAD_DIM, (h + 1) * HEAD_DIM)
        v_ext = _with_ones_columns(v_ref[0, :, hs])
        for qb in reversed(range(seq // tq)):
            lo, hi = qb * tq, (qb + 1) * tq
            q = q_ref[0, lo:hi, hs]
            zero = jnp.zeros_like(q)
            q2 = jnp.concatenate([jnp.where(lane < DIFF_QK_DIM, q, zero),
                                  jnp.where(lane >= DIFF_QK_DIM, q, zero)], axis=0)
            s = lax.dot_general(q2, k_ref[0, :hi, hs], (((1,), (1,)), ((), ())),
                                preferred_element_type=F32)
            s_diag = jnp.where(visible, s[:, lo:], NEG)
            m = jnp.max(s_diag, axis=-1, keepdims=True)
            if qb > 0:
                m = jnp.maximum(m, jnp.max(s[:, :lo], axis=-1, keepdims=True))
            p = jnp.exp2(s_diag - m)
            if qb > 0:
                p = jnp.concatenate([jnp.exp2(s[:, :lo] - m), p], axis=1)
            r = jnp.dot(p.astype(BF16), v_ext[:hi], preferred_element_type=F32)
            o = r[:, :HEAD_DIM] / r[:, HEAD_DIM:]
            d = o[:tq] - lam * o[tq:]
            d = d * lax.rsqrt(jnp.mean(d * d, axis=-1, keepdims=True) + SUBLN_EPS) * g
            o_ref[0, lo:hi, hs] = (d * (1.0 - lambda_init)).astype(o_ref.dtype)


def _diff_attention(proj, lam_vecs, subln_g, *, n_heads, col0_q, col0_k, col0_v, lambda_init,
                    ride_along=(), tq=256, heads=2):
    b, s, _ = proj.shape
    assert s % tq == 0 and n_heads % heads == 0
    assert col0_q % heads == 0 and col0_k % heads == 0 and col0_v % heads == 0
    kern = functools.partial(_diff_attn_kernel, tq=tq, heads=heads, lambda_init=lambda_init)
    w = heads * HEAD_DIM
    steps_per_row = n_heads // heads
    ride_in, ride_out, ride_shapes = _ride_along_specs(
        ride_along, b * steps_per_row, lambda bi, h: bi * steps_per_row + h)
    return pl.pallas_call(
        kern,
        grid=(b, steps_per_row),
        in_specs=[pl.BlockSpec(lam_vecs.shape, lambda bi, h: (0, 0)),
                  pl.BlockSpec((1, HEAD_DIM), lambda bi, h: (0, 0)),
                  pl.BlockSpec((1, s, w), lambda bi, h: (bi, 0, col0_q // heads + h)),
                  pl.BlockSpec((1, s, w), lambda bi, h: (bi, 0, col0_k // heads + h)),
                  pl.BlockSpec((1, s, w), lambda bi, h: (bi, 0, col0_v // heads + h))] + ride_in,
        out_specs=[pl.BlockSpec((1, s, w), lambda bi, h: (bi, 0, h))] + ride_out,
        out_shape=[jax.ShapeDtypeStruct((b, s, n_heads * HEAD_DIM), BF16)] + ride_shapes,
        compiler_params=pltpu.CompilerParams(
            dimension_semantics=("parallel", "parallel"), vmem_limit_bytes=VMEM_LIMIT),
        name="diff_attn",
    )(lam_vecs, subln_g, proj, proj, proj, *ride_along)


def _dilated_bias_table(seq, tq):
    dist = np.arange(tq)[:, None] + (seq - tq) - np.arange(seq)[None, :]
    count = np.zeros(dist.shape, np.int64)
    for window, dilation in DILATION_CONFIGS:
        count += (dist >= 0) & (dist <= window) & (dist % dilation == 0)
    return np.where(count > 0, np.log2(np.maximum(count, 1)), NEG).astype(np.float32)


def _dil_attn_kernel(bias_ref, q_ref, k_ref, v_ref, *rest, tq, heads):
    n_ride = (len(rest) - 1) // 2
    o_ref = rest[n_ride]
    _cast_ride_along(rest[:n_ride], rest[n_ride + 1:])
    seq = q_ref.shape[1]
    for h in range(heads):
        hs = slice(h * HEAD_DIM, (h + 1) * HEAD_DIM)
        v_ext = _with_ones_columns(v_ref[0, :, hs])
        for qb in reversed(range(seq // tq)):
            lo, hi = qb * tq, (qb + 1) * tq
            s = lax.dot_general(q_ref[0, lo:hi, hs], k_ref[0, :hi, hs], (((1,), (1,)), ((), ())),
                                preferred_element_type=F32)
            s = s + bias_ref[:, seq - hi:]
            p = jnp.exp2(s - jnp.max(s, axis=-1, keepdims=True))
            r = jnp.dot(p.astype(BF16), v_ext[:hi], preferred_element_type=F32)
            o_ref[0, lo:hi, hs] = (r[:, :HEAD_DIM] / r[:, HEAD_DIM:]).astype(o_ref.dtype)


def _dilated_attention(proj, *, n_heads, col0_q, col0_k, col0_v, ride_along=(), tq=256, heads=2):
    b, s, _ = proj.shape
    assert s % tq == 0 and n_heads % heads == 0
    assert col0_q % heads == 0 and col0_k % heads == 0 and col0_v % heads == 0
    assert all(s % d == 0 for _, d in DILATION_CONFIGS)
    bias = jnp.asarray(_dilated_bias_table(s, tq))
    w = heads * HEAD_DIM
    steps_per_row = n_heads // heads
    ride_in, ride_out, ride_shapes = _ride_along_specs(
        ride_along, b * steps_per_row, lambda bi, h: bi * steps_per_row + h)
    return pl.pallas_call(
        functools.partial(_dil_attn_kernel, tq=tq, heads=heads),
        grid=(b, steps_per_row),
        in_specs=[pl.BlockSpec(bias.shape, lambda bi, h: (0, 0)),
                  pl.BlockSpec((1, s, w), lambda bi, h: (bi, 0, col0_q // heads + h)),
                  pl.BlockSpec((1, s, w), lambda bi, h: (bi, 0, col0_k // heads + h)),
                  pl.BlockSpec((1, s, w), lambda bi, h: (bi, 0, col0_v // heads + h))] + ride_in,
        out_specs=[pl.BlockSpec((1, s, w), lambda bi, h: (bi, 0, h))] + ride_out,
        out_shape=[jax.ShapeDtypeStruct((b, s, n_heads * HEAD_DIM), BF16)] + ride_shapes,
        compiler_params=pltpu.CompilerParams(
            dimension_semantics=("parallel", "parallel"), vmem_limit_bytes=VMEM_LIMIT),
        name="dilated_attn",
    )(bias, proj, proj, proj, *ride_along)


def _layer_norm_rows(y, g, b):
    mu = jnp.mean(y, axis=-1, keepdims=True)
    yc = y - mu
    var = jnp.mean(yc * yc, axis=-1, keepdims=True)
    return yc * lax.rsqrt(var + LN_EPS) * g + b


def _out_ln_kernel(oa_ref, ob_ref, wa_ref, wb_ref, x_ref, g_ref, b_ref, y_ref, yb_ref,
                   *, alpha, sub_rows):
    for r0 in range(0, x_ref.shape[0], sub_rows):
        rows = slice(r0, r0 + sub_rows)
        attn = jnp.dot(oa_ref[rows, :], wa_ref[...], preferred_element_type=F32)
        attn = attn + jnp.dot(ob_ref[rows, :], wb_ref[...], preferred_element_type=F32)
        y = _layer_norm_rows(alpha * x_ref[rows, :] + attn, g_ref[...], b_ref[...])
        y_ref[rows, :] = y
        yb_ref[rows, :] = y.astype(yb_ref.dtype)


def _out_proj_ln(o_a, o_b, w_out_bf16, x2d, g, b, *, alpha, tm=512, sub_rows=256):
    m, d = x2d.shape
    ka, kb = o_a.shape[1], o_b.shape[1]
    assert m % tm == 0 and tm % sub_rows == 0 and w_out_bf16.shape == (ka + kb, d) and ka == kb
    const = dict(pipeline_mode=pl.Buffered(1))
    return pl.pallas_call(
        functools.partial(_out_ln_kernel, alpha=alpha, sub_rows=sub_rows),
        grid=(m // tm,),
        in_specs=[pl.BlockSpec((tm, ka), lambda i: (i, 0)),
                  pl.BlockSpec((tm, kb), lambda i: (i, 0)),
                  pl.BlockSpec((ka, d), lambda i: (0, 0), **const),
                  pl.BlockSpec((kb, d), lambda i: (1, 0), **const),
                  pl.BlockSpec((tm, d), lambda i: (i, 0)),
                  pl.BlockSpec((1, d), lambda i: (0, 0)),
                  pl.BlockSpec((1, d), lambda i: (0, 0))],
        out_specs=[pl.BlockSpec((tm, d), lambda i: (i, 0)), pl.BlockSpec((tm, d), lambda i: (i, 0))],
        out_shape=[jax.ShapeDtypeStruct((m, d), F32), jax.ShapeDtypeStruct((m, d), BF16)],
        compiler_params=pltpu.CompilerParams(
            dimension_semantics=("parallel",), vmem_limit_bytes=VMEM_LIMIT),
        name="out_proj_ln",
    )(o_a, o_b, w_out_bf16, w_out_bf16, x2d, g, b)


def _ffn_kernel(x_ref, xb_ref, w1_ref, w2_ref, g_ref, b_ref, o_ref, acc_ref, *, alpha, sub_rows):
    f = pl.program_id(1)
    last = pl.num_programs(1) - 1

    def mlp_chunk(rows):
        h = jnp.dot(xb_ref[rows, :], w1_ref[...], preferred_element_type=F32)
        h = jnp.square(jnp.maximum(h, 0.0)).astype(BF16)
        return jnp.dot(h, w2_ref[...], preferred_element_type=F32)

    everything = slice(None)

    @pl.when(f == 0)
    def _():
        acc_ref[...] = alpha * x_ref[...] + mlp_chunk(everything)

    @pl.when((f > 0) & (f < last))
    def _():
        acc_ref[...] += mlp_chunk(everything)

    @pl.when(f == last)
    def _():
        for r0 in range(0, x_ref.shape[0], sub_rows):
            rows = slice(r0, r0 + sub_rows)
            y = acc_ref[rows, :] + mlp_chunk(rows)
            o_ref[rows, :] = _layer_norm_rows(y, g_ref[...], b_ref[...])


def _ffn_ln(x2d, xb2d, w1_bf16, w2_bf16, g, b, *, alpha, tm=512, tf=1024, sub_rows=256):
    m, d = x2d.shape
    dff = w1_bf16.shape[1]
    assert m % tm == 0 and tm % sub_rows == 0 and dff % tf == 0 and dff // tf >= 2
    assert xb2d.shape == x2d.shape
    return pl.pallas_call(
        functools.partial(_ffn_kernel, alpha=alpha, sub_rows=sub_rows),
        grid=(m // tm, dff // tf),
        in_specs=[pl.BlockSpec((tm, d), lambda i, f: (i, 0)),
                  pl.BlockSpec((tm, d), lambda i, f: (i, 0)),
                  pl.BlockSpec((d, tf), lambda i, f: (0, f)),
                  pl.BlockSpec((tf, d), lambda i, f: (f, 0)),
                  pl.BlockSpec((1, d), lambda i, f: (0, 0)),
                  pl.BlockSpec((1, d), lambda i, f: (0, 0))],
        out_specs=pl.BlockSpec((tm, d), lambda i, f: (i, 0)),
        out_shape=jax.ShapeDtypeStruct((m, d), F32),
        scratch_shapes=[pltpu.VMEM((tm, d), F32)],
        compiler_params=pltpu.CompilerParams(
            dimension_semantics=("parallel", "arbitrary"), vmem_limit_bytes=VMEM_LIMIT),
        name="ffn_ln",
    )(x2d, xb2d, w1_bf16, w2_bf16, g, b)


def kernel(x, w_in, lambda_q1, lambda_k1, lambda_q2, lambda_k2, subln_g, w_out, ln1_g, ln1_b,
           w_ff1, w_ff2, ln2_g, ln2_b):
    bsz, seq, d_model = x.shape
    depth = w_in.shape[0]
    n_heads = d_model // (2 * HEAD_DIM)
    width = n_heads * HEAD_DIM
    alpha = (2.0 * depth) ** 0.25
    blocks_per_group = width // HEAD_DIM

    h2d = x.reshape(bsz * seq, d_model)
    for l in range(depth):
        lambda_init = 0.8 - 0.6 * math.exp(-0.3 * l)
        proj = _projection(h2d, w_in[l].astype(BF16), seq).reshape(bsz, seq, 6 * width)
        lam_vecs = jnp.stack([lambda_q1[l], lambda_k1[l], lambda_q2[l], lambda_k2[l]]).astype(F32)
        o_a, w1_b = _diff_attention(
            proj, lam_vecs, subln_g[l].astype(F32).reshape(1, HEAD_DIM), n_heads=n_heads,
            col0_q=0, col0_k=blocks_per_group, col0_v=2 * blocks_per_group,
            lambda_init=lambda_init, ride_along=(w_ff1[l],))
        o_b, w2_b, wo_b = _dilated_attention(
            proj, n_heads=n_heads, col0_q=3 * blocks_per_group, col0_k=4 * blocks_per_group,
            col0_v=5 * blocks_per_group, ride_along=(w_ff2[l], w_out[l]))
        h2d, hb2d = _out_proj_ln(o_a.reshape(bsz * seq, width), o_b.reshape(bsz * seq, width),
                                 wo_b, h2d,
                                 ln1_g[l].reshape(1, d_model), ln1_b[l].reshape(1, d_model), alpha=alpha)
        h2d = _ffn_ln(h2d, hb2d, w1_b, w2_b,
                      ln2_g[l].reshape(1, d_model), ln2_b[l].reshape(1, d_model), alpha=alpha)
    return h2d.reshape(bsz, seq, d_model)
```

```python
import functools
import math

import numpy as np
import jax
import jax.numpy as jnp
from jax import lax
from jax.experimental import pallas as pl
from jax.experimental.pallas import tpu as pltpu

F32 = jnp.float32
BF16 = jnp.bfloat16

HEAD_DIM = 128
DIFF_QK_DIM = 64
ROPE_THETA = 10000.0
LN_EPS = 1e-5
SUBLN_EPS = 1e-5
NEG = -1e30
LOG2E = 1.4426950408889634
DILATION_CONFIGS = ((128, 1), (512, 4), (2048, 16))

V7X_LANES = 128
VMEM_LIMIT = 56 * 1024 * 1024
FFN_VMEM_LIMIT = 62 * 1024 * 1024


def _proj_kernel(x_ref, w_ref, cosa_ref, sina_ref, cosb_ref, sinb_ref, o_ref,
                 *, group_cols, qa_scale, qb_scale):
    xb = x_ref[...].astype(BF16)
    cos_a, sin_a = cosa_ref[...], sina_ref[...]
    cos_b, sin_b = cosb_ref[...], sinb_ref[...]
    lane = lax.broadcasted_iota(jnp.int32, cos_a.shape, 1)
    first_half = (lane % DIFF_QK_DIM) < (DIFF_QK_DIM // 2)

    def rope_a(t):
        rot = jnp.where(first_half, pltpu.roll(t, 96, 1), pltpu.roll(t, 32, 1))
        return t * cos_a + rot * sin_a

    def rope_b(t):
        return t * cos_b + pltpu.roll(t, 64, 1) * sin_b

    groups = ((rope_a, qa_scale), (rope_a, None), (None, None),
              (rope_b, qb_scale), (rope_b, None), (None, None))
    for gi, (rope, scale) in enumerate(groups):
        c0 = gi * group_cols
        acc = jnp.dot(xb, w_ref[:, c0:c0 + group_cols], preferred_element_type=F32)
        for h in range(group_cols // HEAD_DIM):
            y = acc[:, h * HEAD_DIM:(h + 1) * HEAD_DIM]
            if rope is not None:
                y = rope(y)
            if scale is not None:
                y = y * scale
            o_ref[:, c0 + h * HEAD_DIM:c0 + (h + 1) * HEAD_DIM] = y.astype(o_ref.dtype)


def _rope_tables(seq):
    def ang(dim):
        inv = 1.0 / (ROPE_THETA ** (jnp.arange(0, dim, 2, dtype=F32) / dim))
        return jnp.arange(seq, dtype=F32)[:, None] * inv[None, :]
    a = ang(DIFF_QK_DIM)
    ca, sa = jnp.cos(a), jnp.sin(a)
    cos_a = jnp.concatenate([ca, ca, ca, ca], axis=-1)
    sin_a = jnp.concatenate([-sa, sa, -sa, sa], axis=-1)
    b = ang(HEAD_DIM)
    cb, sb = jnp.cos(b), jnp.sin(b)
    cos_b = jnp.concatenate([cb, cb], axis=-1)
    sin_b = jnp.concatenate([-sb, sb], axis=-1)
    return cos_a, sin_a, cos_b, sin_b


def _projection(x2d, w_bf16, seq, *, tm=512):
    m, k = x2d.shape
    n = w_bf16.shape[1]
    assert m % tm == 0 and seq % tm == 0 and n % (6 * HEAD_DIM) == 0
    cos_a, sin_a, cos_b, sin_b = _rope_tables(seq)
    pos_blocks = seq // tm
    tab_spec = pl.BlockSpec((tm, HEAD_DIM), lambda i: (i % pos_blocks, 0))
    kern = functools.partial(
        _proj_kernel, group_cols=n // 6,
        qa_scale=DIFF_QK_DIM ** -0.5 * LOG2E, qb_scale=HEAD_DIM ** -0.5 * LOG2E)
    return pl.pallas_call(
        kern,
        grid=(m // tm,),
        in_specs=[pl.BlockSpec((tm, k), lambda i: (i, 0)),
                  pl.BlockSpec((k, n), lambda i: (0, 0), pipeline_mode=pl.Buffered(1)),
                  tab_spec, tab_spec, tab_spec, tab_spec],
        out_specs=pl.BlockSpec((tm, n), lambda i: (i, 0)),
        out_shape=jax.ShapeDtypeStruct((m, n), BF16),
        compiler_params=pltpu.CompilerParams(
            dimension_semantics=("parallel",), vmem_limit_bytes=VMEM_LIMIT),
        name="proj_rope",
    )(x2d, w_bf16, cos_a, sin_a, cos_b, sin_b)


def _with_ones_columns(v):
    return jnp.concatenate([v, jnp.ones_like(v)], axis=1)


def _ride_along_specs(weights, n_steps, step_index):
    in_specs, out_specs, out_shapes = [], [], []
    for w in weights:
        rows, cols = w.shape
        assert rows % n_steps == 0 and (rows // n_steps) % 16 == 0
        spec = pl.BlockSpec((rows // n_steps, cols), lambda *idx: (step_index(*idx), 0))
        in_specs.append(spec)
        out_specs.append(spec)
        out_shapes.append(jax.ShapeDtypeStruct(w.shape, BF16))
    return in_specs, out_specs, out_shapes


def _cast_ride_along(src_refs, dst_refs):
    for src, dst in zip(src_refs, dst_refs):
        dst[...] = src[...].astype(dst.dtype)


def _diff_attn_kernel(lam_ref, g_ref, q_ref, k_ref, v_ref, *rest, tq, heads, lambda_init):
    n_ride = (len(rest) - 1) // 2
    o_ref = rest[n_ride]
    _cast_ride_along(rest[:n_ride], rest[n_ride + 1:])
    seq = q_ref.shape[1]
    lv = lam_ref[...]
    lam = (jnp.exp(jnp.sum(lv[0:1] * lv[1:2], axis=-1, keepdims=True))
           - jnp.exp(jnp.sum(lv[2:3] * lv[3:4], axis=-1, keepdims=True)) + lambda_init)
    g = g_ref[...]

    lane = lax.broadcasted_iota(jnp.int32, (tq, HEAD_DIM), 1)
    row = lax.broadcasted_iota(jnp.int32, (2 * tq, tq), 0)
    col = lax.broadcasted_iota(jnp.int32, (2 * tq, tq), 1)
    visible = col <= jnp.where(row >= tq, row - tq, row)

    for h in range(heads):
        hs = slice(h * HEAD_DIM, (h + 1) * HEAD_DIM)
        v_ext = _with_ones_columns(v_ref[0, :, hs])
        for qb in reversed(range(seq // tq)):
            lo, hi = qb * tq, (qb + 1) * tq
            q = q_ref[0, lo:hi, hs]
            zero = jnp.zeros_like(q)
            q2 = jnp.concatenate([jnp.where(lane < DIFF_QK_DIM, q, zero),
                                  jnp.where(lane >= DIFF_QK_DIM, q, zero)], axis=0)
            s = lax.dot_general(q2, k_ref[0, :hi, hs], (((1,), (1,)), ((), ())),
                                preferred_element_type=F32)
            s_diag = jnp.where(visible, s[:, lo:], NEG)
            m = jnp.max(s_diag, axis=-1, keepdims=True)
            if qb > 0:
                m = jnp.maximum(m, jnp.max(s[:, :lo], axis=-1, keepdims=True))
            p = jnp.exp2(s_diag - m)
            if qb > 0:
                p = jnp.concatenate([jnp.exp2(s[:, :lo] - m), p], axis=1)
            r = jnp.dot(p.astype(BF16), v_ext[:hi], preferred_element_type=F32)
            o = r[:, :HEAD_DIM] / r[:, HEAD_DIM:]
            d = o[:tq] - lam * o[tq:]
            d = d * lax.rsqrt(jnp.mean(d * d, axis=-1, keepdims=True) + SUBLN_EPS) * g
            o_ref[0, lo:hi, hs] = (d * (1.0 - lambda_init)).astype(o_ref.dtype)


def _diff_attention(proj, lam_vecs, subln_g, *, n_heads, col0_q, col0_k, col0_v, lambda_init,
                    ride_along=(), tq=256, heads=2):
    b, s, _ = proj.shape
    assert s % tq == 0 and n_heads % heads == 0
    assert col0_q % heads == 0 and col0_k % heads == 0 and col0_v % heads == 0
    kern = functools.partial(_diff_attn_kernel, tq=tq, heads=heads, lambda_init=lambda_init)
    w = heads * HEAD_DIM
    steps_per_row = n_heads // heads
    ride_in, ride_out, ride_shapes = _ride_along_specs(
        ride_along, b * steps_per_row, lambda bi, h: bi * steps_per_row + h)
    return pl.pallas_call(
        kern,
        grid=(b, steps_per_row),
        in_specs=[pl.BlockSpec(lam_vecs.shape, lambda bi, h: (0, 0)),
                  pl.BlockSpec((1, HEAD_DIM), lambda bi, h: (0, 0)),
                  pl.BlockSpec((1, s, w), lambda bi, h: (bi, 0, col0_q // heads + h)),
                  pl.BlockSpec((1, s, w), lambda bi, h: (bi, 0, col0_k // heads + h)),
                  pl.BlockSpec((1, s, w), lambda bi, h: (bi, 0, col0_v // heads + h))] + ride_in,
        out_specs=[pl.BlockSpec((1, s, w), lambda bi, h: (bi, 0, h))] + ride_out,
        out_shape=[jax.ShapeDtypeStruct((b, s, n_heads * HEAD_DIM), BF16)] + ride_shapes,
        compiler_params=pltpu.CompilerParams(
            dimension_semantics=("parallel", "parallel"), vmem_limit_bytes=VMEM_LIMIT),
        name="diff_attn",
    )(lam_vecs, subln_g, proj, proj, proj, *ride_along)


def _dilated_bias_table(seq, tq):
    dist = np.arange(tq)[:, None] + (seq - tq) - np.arange(seq)[None, :]
    count = np.zeros(dist.shape, np.int64)
    for window, dilation in DILATION_CONFIGS:
        count += (dist >= 0) & (dist <= window) & (dist % dilation == 0)
    return np.where(count > 0, np.log2(np.maximum(count, 1)), NEG).astype(np.float32)


def _dil_attn_kernel(bias_ref, q_ref, k_ref, v_ref, *rest, tq, heads):
    n_ride = (len(rest) - 1) // 2
    o_ref = rest[n_ride]
    _cast_ride_along(rest[:n_ride], rest[n_ride + 1:])
    seq = q_ref.shape[1]
    for h in range(heads):
        hs = slice(h * HEAD_DIM, (h + 1) * HEAD_DIM)
        v_ext = _with_ones_columns(v_ref[0, :, hs])
        for qb in reversed(range(seq // tq)):
            lo, hi = qb * tq, (qb + 1) * tq
            s = lax.dot_general(q_ref[0, lo:hi, hs], k_ref[0, :hi, hs], (((1,), (1,)), ((), ())),
                                preferred_element_type=F32)
            s = s + bias_ref[:, seq - hi:]
            p = jnp.exp2(s - jnp.max(s, axis=-1, keepdims=True))
            r = jnp.dot(p.astype(BF16), v_ext[:hi], preferred_element_type=F32)
            o_ref[0, lo:hi, hs] = (r[:, :HEAD_DIM] / r[:, HEAD_DIM:]).astype(o_ref.dtype)


def _dilated_attention(proj, *, n_heads, col0_q, col0_k, col0_v, ride_along=(), tq=256, heads=2):
    b, s, _ = proj.shape
    assert s % tq == 0 and n_heads % heads == 0
    assert col0_q % heads == 0 and col0_k % heads == 0 and col0_v % heads == 0
    assert all(s % d == 0 for _, d in DILATION_CONFIGS)
    bias = jnp.asarray(_dilated_bias_table(s, tq))
    w = heads * HEAD_DIM
    steps_per_row = n_heads // heads
    ride_in, ride_out, ride_shapes = _ride_along_specs(
        ride_along, b * steps_per_row, lambda bi, h: bi * steps_per_row + h)
    return pl.pallas_call(
        functools.partial(_dil_attn_kernel, tq=tq, heads=heads),
        grid=(b, steps_per_row),
        in_specs=[pl.BlockSpec(bias.shape, lambda bi, h: (0, 0)),
                  pl.BlockSpec((1, s, w), lambda bi, h: (bi, 0, col0_q // heads + h)),
                  pl.BlockSpec((1, s, w), lambda bi, h: (bi, 0, col0_k // heads + h)),
                  pl.BlockSpec((1, s, w), lambda bi, h: (bi, 0, col0_v // heads + h))] + ride_in,
        out_specs=[pl.BlockSpec((1, s, w), lambda bi, h: (bi, 0, h))] + ride_out,
        out_shape=[jax.ShapeDtypeStruct((b, s, n_heads * HEAD_DIM), BF16)] + ride_shapes,
        compiler_params=pltpu.CompilerParams(
            dimension_semantics=("parallel", "parallel"), vmem_limit_bytes=VMEM_LIMIT),
        name="dilated_attn",
    )(bias, proj, proj, proj, *ride_along)


def _layer_norm_rows(y, g, b):
    mu = jnp.mean(y, axis=-1, keepdims=True)
    yc = y - mu
    var = jnp.mean(yc * yc, axis=-1, keepdims=True)
    return yc * lax.rsqrt(var + LN_EPS) * g + b


def _out_ln_kernel(oa_ref, ob_ref, wa_ref, wb_ref, x_ref, g_ref, b_ref, y_ref, yb_ref,
                   *, alpha, sub_rows):
    for r0 in range(0, x_ref.shape[0], sub_rows):
        rows = slice(r0, r0 + sub_rows)
        attn = jnp.dot(oa_ref[rows, :], wa_ref[...], preferred_element_type=F32)
        attn = attn + jnp.dot(ob_ref[rows, :], wb_ref[...], preferred_element_type=F32)
        y = _layer_norm_rows(alpha * x_ref[rows, :] + attn, g_ref[...], b_ref[...])
        y_ref[rows, :] = y
        yb_ref[rows, :] = y.astype(yb_ref.dtype)


def _out_proj_ln(o_a, o_b, w_out_bf16, x2d, g, b, *, alpha, tm=512, sub_rows=256):
    m, d = x2d.shape
    ka, kb = o_a.shape[1], o_b.shape[1]
    assert m % tm == 0 and tm % sub_rows == 0 and w_out_bf16.shape == (ka + kb, d) and ka == kb
    const = dict(pipeline_mode=pl.Buffered(1))
    return pl.pallas_call(
        functools.partial(_out_ln_kernel, alpha=alpha, sub_rows=sub_rows),
        grid=(m // tm,),
        in_specs=[pl.BlockSpec((tm, ka), lambda i: (i, 0)),
                  pl.BlockSpec((tm, kb), lambda i: (i, 0)),
                  pl.BlockSpec((ka, d), lambda i: (0, 0), **const),
                  pl.BlockSpec((kb, d), lambda i: (1, 0), **const),
                  pl.BlockSpec((tm, d), lambda i: (i, 0)),
                  pl.BlockSpec((1, d), lambda i: (0, 0)),
                  pl.BlockSpec((1, d), lambda i: (0, 0))],
        out_specs=[pl.BlockSpec((tm, d), lambda i: (i, 0)), pl.BlockSpec((tm, d), lambda i: (i, 0))],
        out_shape=[jax.ShapeDtypeStruct((m, d), F32), jax.ShapeDtypeStruct((m, d), BF16)],
        compiler_params=pltpu.CompilerParams(
            dimension_semantics=("parallel",), vmem_limit_bytes=VMEM_LIMIT),
        name="out_proj_ln",
    )(o_a, o_b, w_out_bf16, w_out_bf16, x2d, g, b)


def _ffn_kernel(x_ref, xb_ref, w1_ref, w2_ref, g_ref, b_ref, o_ref, *, alpha, sub_rows):
    f = pl.program_id(1)
    last = pl.num_programs(1) - 1

    def mlp_chunk(rows):
        h = jnp.dot(xb_ref[rows, :], w1_ref[...], preferred_element_type=F32)
        h = jnp.square(jnp.maximum(h, 0.0)).astype(BF16)
        return jnp.dot(h, w2_ref[...], preferred_element_type=F32)

    everything = slice(None)

    @pl.when(f == 0)
    def _():
        o_ref[...] = alpha * x_ref[...] + mlp_chunk(everything)

    @pl.when((f > 0) & (f < last))
    def _():
        o_ref[...] += mlp_chunk(everything)

    @pl.when(f == last)
    def _():
        for r0 in range(0, x_ref.shape[0], sub_rows):
            rows = slice(r0, r0 + sub_rows)
            y = o_ref[rows, :] + mlp_chunk(rows)
            o_ref[rows, :] = _layer_norm_rows(y, g_ref[...], b_ref[...])


def _ffn_ln(x2d, xb2d, w1_bf16, w2_bf16, g, b, *, alpha, tm=512, tf=2048, sub_rows=256):
    m, d = x2d.shape
    dff = w1_bf16.shape[1]
    assert m % tm == 0 and tm % sub_rows == 0 and dff % tf == 0 and dff // tf >= 2
    assert xb2d.shape == x2d.shape
    return pl.pallas_call(
        functools.partial(_ffn_kernel, alpha=alpha, sub_rows=sub_rows),
        grid=(m // tm, dff // tf),
        in_specs=[pl.BlockSpec((tm, d), lambda i, f: (i, 0)),
                  pl.BlockSpec((tm, d), lambda i, f: (i, 0)),
                  pl.BlockSpec((d, tf), lambda i, f: (0, f)),
                  pl.BlockSpec((tf, d), lambda i, f: (f, 0)),
                  pl.BlockSpec((1, d), lambda i, f: (0, 0)),
                  pl.BlockSpec((1, d), lambda i, f: (0, 0))],
        out_specs=pl.BlockSpec((tm, d), lambda i, f: (i, 0)),
        out_shape=jax.ShapeDtypeStruct((m, d), F32),
        compiler_params=pltpu.CompilerParams(
            dimension_semantics=("parallel", "arbitrary"), vmem_limit_bytes=FFN_VMEM_LIMIT),
        name="ffn_ln",
    )(x2d, xb2d, w1_bf16, w2_bf16, g, b)


def kernel(x, w_in, lambda_q1, lambda_k1, lambda_q2, lambda_k2, subln_g, w_out, ln1_g, ln1_b,
           w_ff1, w_ff2, ln2_g, ln2_b):
    bsz, seq, d_model = x.shape
    depth = w_in.shape[0]
    n_heads = d_model // (2 * HEAD_DIM)
    width = n_heads * HEAD_DIM
    alpha = (2.0 * depth) ** 0.25
    blocks_per_group = width // HEAD_DIM

    h2d = x.reshape(bsz * seq, d_model)
    for l in range(depth):
        lambda_init = 0.8 - 0.6 * math.exp(-0.3 * l)
        proj = _projection(h2d, w_in[l].astype(BF16), seq).reshape(bsz, seq, 6 * width)
        lam_vecs = jnp.stack([lambda_q1[l], lambda_k1[l], lambda_q2[l], lambda_k2[l]]).astype(F32)
        o_a, w1_b = _diff_attention(
            proj, lam_vecs, subln_g[l].astype(F32).reshape(1, HEAD_DIM), n_heads=n_heads,
            col0_q=0, col0_k=blocks_per_group, col0_v=2 * blocks_per_group,
            lambda_init=lambda_init, ride_along=(w_ff1[l],))
        o_b, w2_b, wo_b = _dilated_attention(
            proj, n_heads=n_heads, col0_q=3 * blocks_per_group, col0_k=4 * blocks_per_group,
            col0_v=5 * blocks_per_group, ride_along=(w_ff2[l], w_out[l]))
        h2d, hb2d = _out_proj_ln(o_a.reshape(bsz * seq, width), o_b.reshape(bsz * seq, width),
                                 wo_b, h2d,
                                 ln1_g[l].reshape(1, d_model), ln1_b[l].reshape(1, d_model), alpha=alpha)
        h2d = _ffn_ln(h2d, hb2d, w1_b, w2_b,
                      ln2_g[l].reshape(1, d_model), ln2_b[l].reshape(1, d_model), alpha=alpha)
    return h2d.reshape(bsz, seq, d_model)
```

```python
import functools
import math

import numpy as np
import jax
import jax.numpy as jnp
from jax import lax
from jax.experimental import pallas as pl
from jax.experimental.pallas import tpu as pltpu

F32 = jnp.float32
BF16 = jnp.bfloat16

HEAD_DIM = 128
DIFF_QK_DIM = 64
ROPE_THETA = 10000.0
LN_EPS = 1e-5
SUBLN_EPS = 1e-5
NEG = -1e30
LOG2E = 1.4426950408889634
DILATION_CONFIGS = ((128, 1), (512, 4), (2048, 16))

VMEM_LIMIT = 56 * 1024 * 1024
FFN_VMEM_LIMIT = 62 * 1024 * 1024


def _proj_kernel(x_ref, w_ref, cosa_ref, sina_ref, cosb_ref, sinb_ref, o_ref,
                 *, group_cols, qa_scale, qb_scale):
    xb = x_ref[...].astype(BF16)
    cos_a, sin_a = cosa_ref[...], sina_ref[...]
    cos_b, sin_b = cosb_ref[...], sinb_ref[...]
    lane = lax.broadcasted_iota(jnp.int32, cos_a.shape, 1)
    first_half = (lane % DIFF_QK_DIM) < (DIFF_QK_DIM // 2)

    def rope_a(t):
        rot = jnp.where(first_half, pltpu.roll(t, 96, 1), pltpu.roll(t, 32, 1))
        return t * cos_a + rot * sin_a

    def rope_b(t):
        return t * cos_b + pltpu.roll(t, 64, 1) * sin_b

    groups = ((rope_a, qa_scale), (rope_a, None), (None, None),
              (rope_b, qb_scale), (rope_b, None), (None, None))
    for gi, (rope, scale) in enumerate(groups):
        c0 = gi * group_cols
        acc = jnp.dot(xb, w_ref[:, c0:c0 + group_cols], preferred_element_type=F32)
        for h in range(group_cols // HEAD_DIM):
            y = acc[:, h * HEAD_DIM:(h + 1) * HEAD_DIM]
            if rope is not None:
                y = rope(y)
            if scale is not None:
                y = y * scale
            o_ref[:, c0 + h * HEAD_DIM:c0 + (h + 1) * HEAD_DIM] = y.astype(o_ref.dtype)


def _rope_tables(seq):
    def ang(dim):
        inv = 1.0 / (ROPE_THETA ** (jnp.arange(0, dim, 2, dtype=F32) / dim))
        return jnp.arange(seq, dtype=F32)[:, None] * inv[None, :]
    a = ang(DIFF_QK_DIM)
    ca, sa = jnp.cos(a), jnp.sin(a)
    cos_a = jnp.concatenate([ca, ca, ca, ca], axis=-1)
    sin_a = jnp.concatenate([-sa, sa, -sa, sa], axis=-1)
    b = ang(HEAD_DIM)
    cb, sb = jnp.cos(b), jnp.sin(b)
    cos_b = jnp.concatenate([cb, cb], axis=-1)
    sin_b = jnp.concatenate([-sb, sb], axis=-1)
    return cos_a, sin_a, cos_b, sin_b


def _projection(x2d, w_bf16, seq, *, tm=512):
    m, k = x2d.shape
    n = w_bf16.shape[1]
    assert m % tm == 0 and seq % tm == 0 and n % (6 * HEAD_DIM) == 0
    cos_a, sin_a, cos_b, sin_b = _rope_tables(seq)
    pos_blocks = seq // tm
    tab_spec = pl.BlockSpec((tm, HEAD_DIM), lambda i: (i % pos_blocks, 0))
    kern = functools.partial(
        _proj_kernel, group_cols=n // 6,
        qa_scale=DIFF_QK_DIM ** -0.5 * LOG2E, qb_scale=HEAD_DIM ** -0.5 * LOG2E)
    return pl.pallas_call(
        kern,
        grid=(m // tm,),
        in_specs=[pl.BlockSpec((tm, k), lambda i: (i, 0)),
                  pl.BlockSpec((k, n), lambda i: (0, 0), pipeline_mode=pl.Buffered(1)),
                  tab_spec, tab_spec, tab_spec, tab_spec],
        out_specs=pl.BlockSpec((tm, n), lambda i: (i, 0)),
        out_shape=jax.ShapeDtypeStruct((m, n), BF16),
        compiler_params=pltpu.CompilerParams(
            dimension_semantics=("parallel",), vmem_limit_bytes=VMEM_LIMIT),
        name="proj_rope",
    )(x2d, w_bf16, cos_a, sin_a, cos_b, sin_b)


def _with_ones_columns(v):
    return jnp.concatenate([v, jnp.ones_like(v)], axis=1)


def _ride_along_specs(weights, n_steps, step_index):
    in_specs, out_specs, out_shapes = [], [], []
    for w in weights:
        rows, cols = w.shape
        assert rows % n_steps == 0 and (rows // n_steps) % 16 == 0
        spec = pl.BlockSpec((rows // n_steps, cols), lambda *idx: (step_index(*idx), 0))
        in_specs.append(spec)
        out_specs.append(spec)
        out_shapes.append(jax.ShapeDtypeStruct(w.shape, BF16))
    return in_specs, out_specs, out_shapes


def _cast_ride_along(src_refs, dst_refs):
    for src, dst in zip(src_refs, dst_refs):
        dst[...] = src[...].astype(dst.dtype)


def _diff_block(q, k_ref, v_ext, hs, lo, hi, visible, lam, g, lambda_init):
    tq = hi - lo
    lane = lax.broadcasted_iota(jnp.int32, q.shape, 1)
    zero = jnp.zeros_like(q)
    q2 = jnp.concatenate([jnp.where(lane < DIFF_QK_DIM, q, zero),
                          jnp.where(lane >= DIFF_QK_DIM, q, zero)], axis=0)
    s = lax.dot_general(q2, k_ref[0, :hi, hs], (((1,), (1,)), ((), ())),
                        preferred_element_type=F32)
    s_diag = jnp.where(visible, s[:, lo:], NEG)
    m = jnp.max(s_diag, axis=-1, keepdims=True)
    if lo > 0:
        m = jnp.maximum(m, jnp.max(s[:, :lo], axis=-1, keepdims=True))
    p = jnp.exp2(s_diag - m)
    if lo > 0:
        p = jnp.concatenate([jnp.exp2(s[:, :lo] - m), p], axis=1)
    r = jnp.dot(p.astype(BF16), v_ext[:hi], preferred_element_type=F32)
    o = r[:, :HEAD_DIM] / r[:, HEAD_DIM:]
    d = o[:tq] - lam * o[tq:]
    d = d * lax.rsqrt(jnp.mean(d * d, axis=-1, keepdims=True) + SUBLN_EPS) * g
    return d * (1.0 - lambda_init)


def _dilated_block(q, k_ref, v_ext, hs, hi, bias):
    s = lax.dot_general(q, k_ref[0, :hi, hs], (((1,), (1,)), ((), ())),
                        preferred_element_type=F32) + bias
    p = jnp.exp2(s - jnp.max(s, axis=-1, keepdims=True))
    r = jnp.dot(p.astype(BF16), v_ext[:hi], preferred_element_type=F32)
    return r[:, :HEAD_DIM] / r[:, HEAD_DIM:]


def _split_ride_along(rest):
    n_ride = (len(rest) - 1) // 2
    _cast_ride_along(rest[:n_ride], rest[n_ride + 1:])
    return rest[n_ride]


def _diff_attn_kernel(lam_ref, g_ref, q_ref, k_ref, v_ref, *rest, tq, heads, lambda_init):
    o_ref = _split_ride_along(rest)
    seq = q_ref.shape[1]
    lv = lam_ref[...]
    lam = (jnp.exp(jnp.sum(lv[0:1] * lv[1:2], axis=-1, keepdims=True))
           - jnp.exp(jnp.sum(lv[2:3] * lv[3:4], axis=-1, keepdims=True)) + lambda_init)
    g = g_ref[...]
    row = lax.broadcasted_iota(jnp.int32, (2 * tq, tq), 0)
    col = lax.broadcasted_iota(jnp.int32, (2 * tq, tq), 1)
    visible = col <= jnp.where(row >= tq, row - tq, row)
    for h in range(heads):
        hs = slice(h * HEAD_DIM, (h + 1) * HEAD_DIM)
        v_ext = _with_ones_columns(v_ref[0, :, hs])
        for qb in reversed(range(seq // tq)):
            lo, hi = qb * tq, (qb + 1) * tq
            o = _diff_block(q_ref[0, lo:hi, hs], k_ref, v_ext, hs, lo, hi, visible, lam, g, lambda_init)
            o_ref[0, lo:hi, hs] = o.astype(o_ref.dtype)


def _dil_attn_kernel(bias_ref, q_ref, k_ref, v_ref, *rest, tq, heads):
    o_ref = _split_ride_along(rest)
    seq = q_ref.shape[1]
    for h in range(heads):
        hs = slice(h * HEAD_DIM, (h + 1) * HEAD_DIM)
        v_ext = _with_ones_columns(v_ref[0, :, hs])
        for qb in reversed(range(seq // tq)):
            lo, hi = qb * tq, (qb + 1) * tq
            o = _dilated_block(q_ref[0, lo:hi, hs], k_ref, v_ext, hs, hi, bias_ref[:, seq - hi:])
            o_ref[0, lo:hi, hs] = o.astype(o_ref.dtype)


def _head_group_call(kern, name, small_inputs, proj, *, first_group, n_heads, heads, ride_along):
    b, s, _ = proj.shape
    assert n_heads % heads == 0
    w = heads * HEAD_DIM
    steps_per_row = n_heads // heads
    ride_in, ride_out, ride_shapes = _ride_along_specs(
        ride_along, b * steps_per_row, lambda bi, h: bi * steps_per_row + h)

    def group(k):
        return pl.BlockSpec((1, s, w), lambda bi, h: (bi, 0, k * steps_per_row + h))

    small_specs = [pl.BlockSpec(a.shape, lambda bi, h, nd=a.ndim: (0,) * nd) for a in small_inputs]
    return pl.pallas_call(
        kern,
        grid=(b, steps_per_row),
        in_specs=small_specs + [group(first_group + k) for k in range(3)] + ride_in,
        out_specs=[pl.BlockSpec((1, s, w), lambda bi, h: (bi, 0, h))] + ride_out,
        out_shape=[jax.ShapeDtypeStruct((b, s, n_heads * HEAD_DIM), BF16)] + ride_shapes,
        compiler_params=pltpu.CompilerParams(
            dimension_semantics=("parallel", "parallel"), vmem_limit_bytes=VMEM_LIMIT),
        name=name,
    )(*small_inputs, proj, proj, proj, *ride_along)


def _diff_attention(proj, lam_vecs, subln_g, *, n_heads, lambda_init, ride_along=(), tq=256, heads=4):
    assert proj.shape[1] % tq == 0
    kern = functools.partial(_diff_attn_kernel, tq=tq, heads=heads, lambda_init=lambda_init)
    return _head_group_call(kern, "diff_attn", [lam_vecs, subln_g], proj, first_group=0,
                            n_heads=n_heads, heads=heads, ride_along=ride_along)


def _dilated_attention(proj, *, n_heads, ride_along=(), tq=256, heads=4):
    s = proj.shape[1]
    assert s % tq == 0 and all(s % d == 0 for _, d in DILATION_CONFIGS)
    bias = jnp.asarray(_dilated_bias_table(s, tq))
    kern = functools.partial(_dil_attn_kernel, tq=tq, heads=heads)
    return _head_group_call(kern, "dilated_attn", [bias], proj, first_group=3,
                            n_heads=n_heads, heads=heads, ride_along=ride_along)


def _dilated_bias_table(seq, tq):
    dist = np.arange(tq)[:, None] + (seq - tq) - np.arange(seq)[None, :]
    count = np.zeros(dist.shape, np.int64)
    for window, dilation in DILATION_CONFIGS:
        count += (dist >= 0) & (dist <= window) & (dist % dilation == 0)
    return np.where(count > 0, np.log2(np.maximum(count, 1)), NEG).astype(np.float32)


def _layer_norm_rows(y, g, b):
    mu = jnp.mean(y, axis=-1, keepdims=True)
    yc = y - mu
    var = jnp.mean(yc * yc, axis=-1, keepdims=True)
    return yc * lax.rsqrt(var + LN_EPS) * g + b


def _out_ln_kernel(oa_ref, ob_ref, wa_ref, wb_ref, x_ref, g_ref, b_ref, y_ref, yb_ref,
                   *, alpha, sub_rows):
    for r0 in range(0, x_ref.shape[0], sub_rows):
        rows = slice(r0, r0 + sub_rows)
        attn = jnp.dot(oa_ref[rows, :], wa_ref[...], preferred_element_type=F32)
        attn = attn + jnp.dot(ob_ref[rows, :], wb_ref[...], preferred_element_type=F32)
        y = _layer_norm_rows(alpha * x_ref[rows, :] + attn, g_ref[...], b_ref[...])
        y_ref[rows, :] = y
        yb_ref[rows, :] = y.astype(yb_ref.dtype)


def _out_proj_ln(o_a, o_b, w_out_bf16, x2d, g, b, *, alpha, tm=512, sub_rows=256):
    m, d = x2d.shape
    ka, kb = o_a.shape[1], o_b.shape[1]
    assert m % tm == 0 and tm % sub_rows == 0 and w_out_bf16.shape == (ka + kb, d) and ka == kb
    const = dict(pipeline_mode=pl.Buffered(1))
    return pl.pallas_call(
        functools.partial(_out_ln_kernel, alpha=alpha, sub_rows=sub_rows),
        grid=(m // tm,),
        in_specs=[pl.BlockSpec((tm, ka), lambda i: (i, 0)),
                  pl.BlockSpec((tm, kb), lambda i: (i, 0)),
                  pl.BlockSpec((ka, d), lambda i: (0, 0), **const),
                  pl.BlockSpec((kb, d), lambda i: (1, 0), **const),
                  pl.BlockSpec((tm, d), lambda i: (i, 0)),
                  pl.BlockSpec((1, d), lambda i: (0, 0)),
                  pl.BlockSpec((1, d), lambda i: (0, 0))],
        out_specs=[pl.BlockSpec((tm, d), lambda i: (i, 0)), pl.BlockSpec((tm, d), lambda i: (i, 0))],
        out_shape=[jax.ShapeDtypeStruct((m, d), F32), jax.ShapeDtypeStruct((m, d), BF16)],
        compiler_params=pltpu.CompilerParams(
            dimension_semantics=("parallel",), vmem_limit_bytes=VMEM_LIMIT),
        name="out_proj_ln",
    )(o_a, o_b, w_out_bf16, w_out_bf16, x2d, g, b)


def _ffn_kernel(x_ref, xb_ref, w1_ref, w2_ref, g_ref, b_ref, o_ref, *, alpha, sub_rows):
    f = pl.program_id(1)
    last = pl.num_programs(1) - 1

    def mlp_chunk(rows):
        h = jnp.dot(xb_ref[rows, :], w1_ref[...], preferred_element_type=F32)
        h = jnp.square(jnp.maximum(h, 0.0)).astype(BF16)
        return jnp.dot(h, w2_ref[...], preferred_element_type=F32)

    everything = slice(None)

    @pl.when(f == 0)
    def _():
        o_ref[...] = alpha * x_ref[...] + mlp_chunk(everything)

    @pl.when((f > 0) & (f < last))
    def _():
        o_ref[...] += mlp_chunk(everything)

    @pl.when(f == last)
    def _():
        for r0 in range(0, x_ref.shape[0], sub_rows):
            rows = slice(r0, r0 + sub_rows)
            y = o_ref[rows, :] + mlp_chunk(rows)
            o_ref[rows, :] = _layer_norm_rows(y, g_ref[...], b_ref[...])


def _ffn_ln(x2d, xb2d, w1_bf16, w2_bf16, g, b, *, alpha, tm=512, tf=2048, sub_rows=256):
    m, d = x2d.shape
    dff = w1_bf16.shape[1]
    assert m % tm == 0 and tm % sub_rows == 0 and dff % tf == 0 and dff // tf >= 2
    assert xb2d.shape == x2d.shape
    return pl.pallas_call(
        functools.partial(_ffn_kernel, alpha=alpha, sub_rows=sub_rows),
        grid=(m // tm, dff // tf),
        in_specs=[pl.BlockSpec((tm, d), lambda i, f: (i, 0)),
                  pl.BlockSpec((tm, d), lambda i, f: (i, 0)),
                  pl.BlockSpec((d, tf), lambda i, f: (0, f)),
                  pl.BlockSpec((tf, d), lambda i, f: (f, 0)),
                  pl.BlockSpec((1, d), lambda i, f: (0, 0)),
                  pl.BlockSpec((1, d), lambda i, f: (0, 0))],
        out_specs=pl.BlockSpec((tm, d), lambda i, f: (i, 0)),
        out_shape=jax.ShapeDtypeStruct((m, d), F32),
        compiler_params=pltpu.CompilerParams(
            dimension_semantics=("parallel", "arbitrary"), vmem_limit_bytes=FFN_VMEM_LIMIT),
        name="ffn_ln",
    )(x2d, xb2d, w1_bf16, w2_bf16, g, b)


def kernel(x, w_in, lambda_q1, lambda_k1, lambda_q2, lambda_k2, subln_g, w_out, ln1_g, ln1_b,
           w_ff1, w_ff2, ln2_g, ln2_b):
    bsz, seq, d_model = x.shape
    depth = w_in.shape[0]
    n_heads = d_model // (2 * HEAD_DIM)
    width = n_heads * HEAD_DIM
    alpha = (2.0 * depth) ** 0.25

    h2d = x.reshape(bsz * seq, d_model)
    for l in range(depth):
        lambda_init = 0.8 - 0.6 * math.exp(-0.3 * l)
        proj = _projection(h2d, w_in[l].astype(BF16), seq).reshape(bsz, seq, 6 * width)
        lam_vecs = jnp.stack([lambda_q1[l], lambda_k1[l], lambda_q2[l], lambda_k2[l]]).astype(F32)
        o_a, w1_b = _diff_attention(
            proj, lam_vecs, subln_g[l].astype(F32).reshape(1, HEAD_DIM), n_heads=n_heads,
            lambda_init=lambda_init, ride_along=(w_ff1[l],))
        o_b, w2_b, wo_b = _dilated_attention(proj, n_heads=n_heads, ride_along=(w_ff2[l], w_out[l]))
        h2d, hb2d = _out_proj_ln(o_a.reshape(bsz * seq, width), o_b.reshape(bsz * seq, width),
                                 wo_b, h2d,
                                 ln1_g[l].reshape(1, d_model), ln1_b[l].reshape(1, d_model), alpha=alpha)
        h2d = _ffn_ln(h2d, hb2d, w1_b, w2_b,
                      ln2_g[l].reshape(1, d_model), ln2_b[l].reshape(1, d_model), alpha=alpha)
    return h2d.reshape(bsz, seq, d_model)
```

```python
import functools
import math

import numpy as np
import jax
import jax.numpy as jnp
from jax import lax
from jax.experimental import pallas as pl
from jax.experimental.pallas import tpu as pltpu

F32 = jnp.float32
BF16 = jnp.bfloat16

HEAD_DIM = 128
DIFF_QK_DIM = 64
ROPE_THETA = 10000.0
LN_EPS = 1e-5
SUBLN_EPS = 1e-5
NEG = -1e30
LOG2E = 1.4426950408889634
DILATION_CONFIGS = ((128, 1), (512, 4), (2048, 16))

VMEM_LIMIT = 56 * 1024 * 1024
FFN_VMEM_LIMIT = 62 * 1024 * 1024


def _proj_kernel(x_ref, w_ref, cosa_ref, sina_ref, cosb_ref, sinb_ref, o_ref,
                 *, group_cols, qa_scale, qb_scale):
    xb = x_ref[...].astype(BF16)
    cos_a, sin_a = cosa_ref[...], sina_ref[...]
    cos_b, sin_b = cosb_ref[...], sinb_ref[...]
    lane = lax.broadcasted_iota(jnp.int32, cos_a.shape, 1)
    first_half = (lane % DIFF_QK_DIM) < (DIFF_QK_DIM // 2)

    def rope_a(t):
        rot = jnp.where(first_half, pltpu.roll(t, 96, 1), pltpu.roll(t, 32, 1))
        return t * cos_a + rot * sin_a

    def rope_b(t):
        return t * cos_b + pltpu.roll(t, 64, 1) * sin_b

    groups = ((rope_a, qa_scale), (rope_a, None), (None, None),
              (rope_b, qb_scale), (rope_b, None), (None, None))
    for gi, (rope, scale) in enumerate(groups):
        c0 = gi * group_cols
        acc = jnp.dot(xb, w_ref[:, c0:c0 + group_cols], preferred_element_type=F32)
        for h in range(group_cols // HEAD_DIM):
            y = acc[:, h * HEAD_DIM:(h + 1) * HEAD_DIM]
            if rope is not None:
                y = rope(y)
            if scale is not None:
                y = y * scale
            o_ref[:, c0 + h * HEAD_DIM:c0 + (h + 1) * HEAD_DIM] = y.astype(o_ref.dtype)


def _rope_tables(seq):
    def ang(dim):
        inv = 1.0 / (ROPE_THETA ** (jnp.arange(0, dim, 2, dtype=F32) / dim))
        return jnp.arange(seq, dtype=F32)[:, None] * inv[None, :]
    a = ang(DIFF_QK_DIM)
    ca, sa = jnp.cos(a), jnp.sin(a)
    cos_a = jnp.concatenate([ca, ca, ca, ca], axis=-1)
    sin_a = jnp.concatenate([-sa, sa, -sa, sa], axis=-1)
    b = ang(HEAD_DIM)
    cb, sb = jnp.cos(b), jnp.sin(b)
    cos_b = jnp.concatenate([cb, cb], axis=-1)
    sin_b = jnp.concatenate([-sb, sb], axis=-1)
    return cos_a, sin_a, cos_b, sin_b


def _projection(x2d, w_bf16, seq, *, tm=512):
    m, k = x2d.shape
    n = w_bf16.shape[1]
    assert m % tm == 0 and seq % tm == 0 and n % (6 * HEAD_DIM) == 0
    cos_a, sin_a, cos_b, sin_b = _rope_tables(seq)
    pos_blocks = seq // tm
    tab_spec = pl.BlockSpec((tm, HEAD_DIM), lambda i: (i % pos_blocks, 0))
    kern = functools.partial(
        _proj_kernel, group_cols=n // 6,
        qa_scale=DIFF_QK_DIM ** -0.5 * LOG2E, qb_scale=HEAD_DIM ** -0.5 * LOG2E)
    return pl.pallas_call(
        kern,
        grid=(m // tm,),
        in_specs=[pl.BlockSpec((tm, k), lambda i: (i, 0)),
                  pl.BlockSpec((k, n), lambda i: (0, 0), pipeline_mode=pl.Buffered(1)),
                  tab_spec, tab_spec, tab_spec, tab_spec],
        out_specs=pl.BlockSpec((tm, n), lambda i: (i, 0)),
        out_shape=jax.ShapeDtypeStruct((m, n), BF16),
        compiler_params=pltpu.CompilerParams(
            dimension_semantics=("parallel",), vmem_limit_bytes=VMEM_LIMIT),
        name="proj_rope",
    )(x2d, w_bf16, cos_a, sin_a, cos_b, sin_b)


def _with_ones_columns(v):
    return jnp.concatenate([v, jnp.ones_like(v)], axis=1)


def _ride_along_specs(weights, n_steps, step_index):
    in_specs, out_specs, out_shapes = [], [], []
    for w in weights:
        rows, cols = w.shape
        assert rows % n_steps == 0 and (rows // n_steps) % 16 == 0
        spec = pl.BlockSpec((rows // n_steps, cols), lambda *idx: (step_index(*idx), 0))
        in_specs.append(spec)
        out_specs.append(spec)
        out_shapes.append(jax.ShapeDtypeStruct(w.shape, BF16))
    return in_specs, out_specs, out_shapes


def _cast_ride_along(src_refs, dst_refs):
    for src, dst in zip(src_refs, dst_refs):
        dst[...] = src[...].astype(dst.dtype)


def _diff_block(q, k_ref, v_ext, hs, lo, hi, visible, lam, g, lambda_init):
    tq = hi - lo
    lane = lax.broadcasted_iota(jnp.int32, q.shape, 1)
    zero = jnp.zeros_like(q)
    q2 = jnp.concatenate([jnp.where(lane < DIFF_QK_DIM, q, zero),
                          jnp.where(lane >= DIFF_QK_DIM, q, zero)], axis=0)
    s = lax.dot_general(q2, k_ref[0, :hi, hs], (((1,), (1,)), ((), ())),
                        preferred_element_type=F32)
    s_diag = jnp.where(visible, s[:, lo:], NEG)
    m = jnp.max(s_diag, axis=-1, keepdims=True)
    if lo > 0:
        m = jnp.maximum(m, jnp.max(s[:, :lo], axis=-1, keepdims=True))
    p = jnp.exp2(s_diag - m)
    if lo > 0:
        p = jnp.concatenate([jnp.exp2(s[:, :lo] - m), p], axis=1)
    r = jnp.dot(p.astype(BF16), v_ext[:hi], preferred_element_type=F32)
    o = r[:, :HEAD_DIM] / r[:, HEAD_DIM:]
    d = o[:tq] - lam * o[tq:]
    d = d * lax.rsqrt(jnp.mean(d * d, axis=-1, keepdims=True) + SUBLN_EPS) * g
    return d * (1.0 - lambda_init)


def _dilated_block(q, k_ref, v_ext, hs, hi, bias):
    s = lax.dot_general(q, k_ref[0, :hi, hs], (((1,), (1,)), ((), ())),
                        preferred_element_type=F32) + bias
    p = jnp.exp2(s - jnp.max(s, axis=-1, keepdims=True))
    r = jnp.dot(p.astype(BF16), v_ext[:hi], preferred_element_type=F32)
    return r[:, :HEAD_DIM] / r[:, HEAD_DIM:]


def _split_ride_along(rest):
    n_ride = (len(rest) - 1) // 2
    _cast_ride_along(rest[:n_ride], rest[n_ride + 1:])
    return rest[n_ride]


def _diff_attn_kernel(lam_ref, g_ref, q_ref, k_ref, v_ref, *rest, tq, heads, lambda_init):
    o_ref = _split_ride_along(rest)
    seq = q_ref.shape[1]
    lv = lam_ref[...]
    lam = (jnp.exp(jnp.sum(lv[0:1] * lv[1:2], axis=-1, keepdims=True))
           - jnp.exp(jnp.sum(lv[2:3] * lv[3:4], axis=-1, keepdims=True)) + lambda_init)
    g = g_ref[...]
    row = lax.broadcasted_iota(jnp.int32, (2 * tq, tq), 0)
    col = lax.broadcasted_iota(jnp.int32, (2 * tq, tq), 1)
    visible = col <= jnp.where(row >= tq, row - tq, row)
    for h in range(heads):
        hs = slice(h * HEAD_DIM, (h + 1) * HEAD_DIM)
        v_ext = _with_ones_columns(v_ref[0, :, hs])
        for qb in reversed(range(seq // tq)):
            lo, hi = qb * tq, (qb + 1) * tq
            o = _diff_block(q_ref[0, lo:hi, hs], k_ref, v_ext, hs, lo, hi, visible, lam, g, lambda_init)
            o_ref[0, lo:hi, hs] = o.astype(o_ref.dtype)


def _dil_attn_kernel(bias_ref, q_ref, k_ref, v_ref, *rest, tq, heads):
    o_ref = _split_ride_along(rest)
    seq = q_ref.shape[1]
    for h in range(heads):
        hs = slice(h * HEAD_DIM, (h + 1) * HEAD_DIM)
        v_ext = _with_ones_columns(v_ref[0, :, hs])
        for qb in reversed(range(seq // tq)):
            lo, hi = qb * tq, (qb + 1) * tq
            o = _dilated_block(q_ref[0, lo:hi, hs], k_ref, v_ext, hs, hi, bias_ref[:, seq - hi:])
            o_ref[0, lo:hi, hs] = o.astype(o_ref.dtype)


def _head_group_call(kern, name, small_inputs, proj, *, first_group, n_heads, heads, ride_along):
    b, s, _ = proj.shape
    assert n_heads % heads == 0
    w = heads * HEAD_DIM
    steps_per_row = n_heads // heads
    ride_in, ride_out, ride_shapes = _ride_along_specs(
        ride_along, b * steps_per_row, lambda bi, h: bi * steps_per_row + h)

    def group(k):
        return pl.BlockSpec((1, s, w), lambda bi, h: (bi, 0, k * steps_per_row + h))

    small_specs = [pl.BlockSpec(a.shape, lambda bi, h, nd=a.ndim: (0,) * nd) for a in small_inputs]
    return pl.pallas_call(
        kern,
        grid=(b, steps_per_row),
        in_specs=small_specs + [group(first_group + k) for k in range(3)] + ride_in,
        out_specs=[pl.BlockSpec((1, s, w), lambda bi, h: (bi, 0, h))] + ride_out,
        out_shape=[jax.ShapeDtypeStruct((b, s, n_heads * HEAD_DIM), BF16)] + ride_shapes,
        compiler_params=pltpu.CompilerParams(
            dimension_semantics=("parallel", "parallel"), vmem_limit_bytes=VMEM_LIMIT),
        name=name,
    )(*small_inputs, proj, proj, proj, *ride_along)


def _diff_attention(proj, lam_vecs, subln_g, *, n_heads, lambda_init, ride_along=(), tq=256, heads=2):
    assert proj.shape[1] % tq == 0
    kern = functools.partial(_diff_attn_kernel, tq=tq, heads=heads, lambda_init=lambda_init)
    return _head_group_call(kern, "diff_attn", [lam_vecs, subln_g], proj, first_group=0,
                            n_heads=n_heads, heads=heads, ride_along=ride_along)


def _dilated_attention(proj, *, n_heads, ride_along=(), tq=256, heads=2):
    s = proj.shape[1]
    assert s % tq == 0 and all(s % d == 0 for _, d in DILATION_CONFIGS)
    bias = jnp.asarray(_dilated_bias_table(s, tq))
    kern = functools.partial(_dil_attn_kernel, tq=tq, heads=heads)
    return _head_group_call(kern, "dilated_attn", [bias], proj, first_group=3,
                            n_heads=n_heads, heads=heads, ride_along=ride_along)


def _dilated_bias_table(seq, tq):
    dist = np.arange(tq)[:, None] + (seq - tq) - np.arange(seq)[None, :]
    count = np.zeros(dist.shape, np.int64)
    for window, dilation in DILATION_CONFIGS:
        count += (dist >= 0) & (dist <= window) & (dist % dilation == 0)
    return np.where(count > 0, np.log2(np.maximum(count, 1)), NEG).astype(np.float32)


def _layer_norm_rows(y, g, b):
    mu = jnp.mean(y, axis=-1, keepdims=True)
    yc = y - mu
    var = jnp.mean(yc * yc, axis=-1, keepdims=True)
    return yc * lax.rsqrt(var + LN_EPS) * g + b


def _out_ln_kernel(oa_ref, ob_ref, wa_ref, wb_ref, x_ref, g_ref, b_ref, y_ref, yb_ref,
                   *, alpha, sub_rows):
    for r0 in range(0, x_ref.shape[0], sub_rows):
        rows = slice(r0, r0 + sub_rows)
        attn = jnp.dot(oa_ref[rows, :], wa_ref[...], preferred_element_type=F32)
        attn = attn + jnp.dot(ob_ref[rows, :], wb_ref[...], preferred_element_type=F32)
        y = _layer_norm_rows(alpha * x_ref[rows, :] + attn, g_ref[...], b_ref[...])
        y_ref[rows, :] = y
        yb_ref[rows, :] = y.astype(yb_ref.dtype)


def _out_proj_ln(o_a, o_b, w_out_bf16, x2d, g, b, *, alpha, tm=512, sub_rows=256):
    m, d = x2d.shape
    ka, kb = o_a.shape[1], o_b.shape[1]
    assert m % tm == 0 and tm % sub_rows == 0 and w_out_bf16.shape == (ka + kb, d) and ka == kb
    const = dict(pipeline_mode=pl.Buffered(1))
    return pl.pallas_call(
        functools.partial(_out_ln_kernel, alpha=alpha, sub_rows=sub_rows),
        grid=(m // tm,),
        in_specs=[pl.BlockSpec((tm, ka), lambda i: (i, 0)),
                  pl.BlockSpec((tm, kb), lambda i: (i, 0)),
                  pl.BlockSpec((ka, d), lambda i: (0, 0), **const),
                  pl.BlockSpec((kb, d), lambda i: (1, 0), **const),
                  pl.BlockSpec((tm, d), lambda i: (i, 0)),
                  pl.BlockSpec((1, d), lambda i: (0, 0)),
                  pl.BlockSpec((1, d), lambda i: (0, 0))],
        out_specs=[pl.BlockSpec((tm, d), lambda i: (i, 0)), pl.BlockSpec((tm, d), lambda i: (i, 0))],
        out_shape=[jax.ShapeDtypeStruct((m, d), F32), jax.ShapeDtypeStruct((m, d), BF16)],
        compiler_params=pltpu.CompilerParams(
            dimension_semantics=("parallel",), vmem_limit_bytes=VMEM_LIMIT),
        name="out_proj_ln",
    )(o_a, o_b, w_out_bf16, w_out_bf16, x2d, g, b)


def _ffn_kernel(x_ref, xb_ref, w1_ref, w2_ref, g_ref, b_ref, o_ref, *, alpha, sub_rows):
    f = pl.program_id(1)
    last = pl.num_programs(1) - 1

    def mlp_chunk(rows):
        h = jnp.dot(xb_ref[rows, :], w1_ref[...], preferred_element_type=F32)
        h = jnp.square(jnp.maximum(h, 0.0)).astype(BF16)
        return jnp.dot(h, w2_ref[...], preferred_element_type=F32)

    everything = slice(None)

    @pl.when(f == 0)
    def _():
        o_ref[...] = alpha * x_ref[...] + mlp_chunk(everything)

    @pl.when((f > 0) & (f < last))
    def _():
        o_ref[...] += mlp_chunk(everything)

    @pl.when(f == last)
    def _():
        for r0 in range(0, x_ref.shape[0], sub_rows):
            rows = slice(r0, r0 + sub_rows)
            y = o_ref[rows, :] + mlp_chunk(rows)
            o_ref[rows, :] = _layer_norm_rows(y, g_ref[...], b_ref[...])


def _ffn_ln(x2d, xb2d, w1_bf16, w2_bf16, g, b, *, alpha, tm=512, tf=2048, sub_rows=256):
    m, d = x2d.shape
    dff = w1_bf16.shape[1]
    assert m % tm == 0 and tm % sub_rows == 0 and dff % tf == 0 and dff // tf >= 2
    assert xb2d.shape == x2d.shape
    return pl.pallas_call(
        functools.partial(_ffn_kernel, alpha=alpha, sub_rows=sub_rows),
        grid=(m // tm, dff // tf),
        in_specs=[pl.BlockSpec((tm, d), lambda i, f: (i, 0)),
                  pl.BlockSpec((tm, d), lambda i, f: (i, 0)),
                  pl.BlockSpec((d, tf), lambda i, f: (0, f)),
                  pl.BlockSpec((tf, d), lambda i, f: (f, 0)),
                  pl.BlockSpec((1, d), lambda i, f: (0, 0)),
                  pl.BlockSpec((1, d), lambda i, f: (0, 0))],
        out_specs=pl.BlockSpec((tm, d), lambda i, f: (i, 0)),
        out_shape=jax.ShapeDtypeStruct((m, d), F32),
        compiler_params=pltpu.CompilerParams(
            dimension_semantics=("parallel", "arbitrary"), vmem_limit_bytes=FFN_VMEM_LIMIT),
        name="ffn_ln",
    )(x2d, xb2d, w1_bf16, w2_bf16, g, b)


def kernel(x, w_in, lambda_q1, lambda_k1, lambda_q2, lambda_k2, subln_g, w_out, ln1_g, ln1_b,
           w_ff1, w_ff2, ln2_g, ln2_b):
    bsz, seq, d_model = x.shape
    depth = w_in.shape[0]
    n_heads = d_model // (2 * HEAD_DIM)
    width = n_heads * HEAD_DIM
    alpha = (2.0 * depth) ** 0.25

    h2d = x.reshape(bsz * seq, d_model)
    for l in range(depth):
        lambda_init = 0.8 - 0.6 * math.exp(-0.3 * l)
        proj = _projection(h2d, w_in[l].astype(BF16), seq).reshape(bsz, seq, 6 * width)
        lam_vecs = jnp.stack([lambda_q1[l], lambda_k1[l], lambda_q2[l], lambda_k2[l]]).astype(F32)
        o_a, w1_b = _diff_attention(
            proj, lam_vecs, subln_g[l].astype(F32).reshape(1, HEAD_DIM), n_heads=n_heads,
            lambda_init=lambda_init, ride_along=(w_ff1[l],))
        o_b, w2_b, wo_b = _dilated_attention(proj, n_heads=n_heads, ride_along=(w_ff2[l], w_out[l]))
        h2d, hb2d = _out_proj_ln(o_a.reshape(bsz * seq, width), o_b.reshape(bsz * seq, width),
                                 wo_b, h2d,
                                 ln1_g[l].reshape(1, d_model), ln1_b[l].reshape(1, d_model), alpha=alpha)
        h2d = _ffn_ln(h2d, hb2d, w1_b, w2_b,
                      ln2_g[l].reshape(1, d_model), ln2_b[l].reshape(1, d_model), alpha=alpha)
    return h2d.reshape(bsz, seq, d_model)
```

```python
import functools
import math

import numpy as np
import jax
import jax.numpy as jnp
from jax import lax
from jax.experimental import pallas as pl
from jax.experimental.pallas import tpu as pltpu

F32 = jnp.float32
BF16 = jnp.bfloat16

HEAD_DIM = 128
DIFF_QK_DIM = 64
ROPE_THETA = 10000.0
LN_EPS = 1e-5
SUBLN_EPS = 1e-5
NEG = -1e30
LOG2E = 1.4426950408889634
DILATION_CONFIGS = ((128, 1), (512, 4), (2048, 16))

VMEM_LIMIT = 56 * 1024 * 1024
FFN_VMEM_LIMIT = 62 * 1024 * 1024


def _proj_kernel(x_ref, w_ref, cosa_ref, sina_ref, cosb_ref, sinb_ref, o_ref,
                 *, group_cols, qa_scale, qb_scale):
    xb = x_ref[...].astype(BF16)
    cos_a, sin_a = cosa_ref[...], sina_ref[...]
    cos_b, sin_b = cosb_ref[...], sinb_ref[...]
    lane = lax.broadcasted_iota(jnp.int32, cos_a.shape, 1)
    first_half = (lane % DIFF_QK_DIM) < (DIFF_QK_DIM // 2)

    def rope_a(t):
        rot = jnp.where(first_half, pltpu.roll(t, 96, 1), pltpu.roll(t, 32, 1))
        return t * cos_a + rot * sin_a

    def rope_b(t):
        return t * cos_b + pltpu.roll(t, 64, 1) * sin_b

    groups = ((rope_a, qa_scale), (rope_a, None), (None, None),
              (rope_b, qb_scale), (rope_b, None), (None, None))
    for gi, (rope, scale) in enumerate(groups):
        c0 = gi * group_cols
        acc = jnp.dot(xb, w_ref[:, c0:c0 + group_cols], preferred_element_type=F32)
        for h in range(group_cols // HEAD_DIM):
            y = acc[:, h * HEAD_DIM:(h + 1) * HEAD_DIM]
            if rope is not None:
                y = rope(y)
            if scale is not None:
                y = y * scale
            o_ref[:, c0 + h * HEAD_DIM:c0 + (h + 1) * HEAD_DIM] = y.astype(o_ref.dtype)


def _rope_tables(seq):
    def ang(dim):
        inv = 1.0 / (ROPE_THETA ** (jnp.arange(0, dim, 2, dtype=F32) / dim))
        return jnp.arange(seq, dtype=F32)[:, None] * inv[None, :]
    a = ang(DIFF_QK_DIM)
    ca, sa = jnp.cos(a), jnp.sin(a)
    cos_a = jnp.concatenate([ca, ca, ca, ca], axis=-1)
    sin_a = jnp.concatenate([-sa, sa, -sa, sa], axis=-1)
    b = ang(HEAD_DIM)
    cb, sb = jnp.cos(b), jnp.sin(b)
    cos_b = jnp.concatenate([cb, cb], axis=-1)
    sin_b = jnp.concatenate([-sb, sb], axis=-1)
    return cos_a, sin_a, cos_b, sin_b


def _projection(x2d, w_bf16, seq, *, tm=512):
    m, k = x2d.shape
    n = w_bf16.shape[1]
    assert m % tm == 0 and seq % tm == 0 and n % (6 * HEAD_DIM) == 0
    cos_a, sin_a, cos_b, sin_b = _rope_tables(seq)
    pos_blocks = seq // tm
    tab_spec = pl.BlockSpec((tm, HEAD_DIM), lambda i: (i % pos_blocks, 0))
    kern = functools.partial(
        _proj_kernel, group_cols=n // 6,
        qa_scale=DIFF_QK_DIM ** -0.5 * LOG2E, qb_scale=HEAD_DIM ** -0.5 * LOG2E)
    return pl.pallas_call(
        kern,
        grid=(m // tm,),
        in_specs=[pl.BlockSpec((tm, k), lambda i: (i, 0)),
                  pl.BlockSpec((k, n), lambda i: (0, 0), pipeline_mode=pl.Buffered(1)),
                  tab_spec, tab_spec, tab_spec, tab_spec],
        out_specs=pl.BlockSpec((tm, n), lambda i: (i, 0)),
        out_shape=jax.ShapeDtypeStruct((m, n), BF16),
        compiler_params=pltpu.CompilerParams(
            dimension_semantics=("parallel",), vmem_limit_bytes=VMEM_LIMIT),
        name="proj_rope",
    )(x2d, w_bf16, cos_a, sin_a, cos_b, sin_b)


def _with_ones_columns(v):
    return jnp.concatenate([v, jnp.ones_like(v)], axis=1)


def _ride_along_specs(weights, n_steps, step_index):
    in_specs, out_specs, out_shapes = [], [], []
    for w in weights:
        rows, cols = w.shape
        assert rows % n_steps == 0 and (rows // n_steps) % 16 == 0
        spec = pl.BlockSpec((rows // n_steps, cols), lambda *idx: (step_index(*idx), 0))
        in_specs.append(spec)
        out_specs.append(spec)
        out_shapes.append(jax.ShapeDtypeStruct(w.shape, BF16))
    return in_specs, out_specs, out_shapes


def _cast_ride_along(src_refs, dst_refs):
    for src, dst in zip(src_refs, dst_refs):
        dst[...] = src[...].astype(dst.dtype)


def _diff_scores(q, k_ref, hs, hi):
    lane = lax.broadcasted_iota(jnp.int32, q.shape, 1)
    zero = jnp.zeros_like(q)
    q2 = jnp.concatenate([jnp.where(lane < DIFF_QK_DIM, q, zero),
                          jnp.where(lane >= DIFF_QK_DIM, q, zero)], axis=0)
    return lax.dot_general(q2, k_ref[0, :hi, hs], (((1,), (1,)), ((), ())),
                           preferred_element_type=F32)


def _diff_finish(s, v_ext, lo, hi, visible, lam, g, lambda_init):
    tq = hi - lo
    s_diag = jnp.where(visible, s[:, lo:], NEG)
    m = jnp.max(s_diag, axis=-1, keepdims=True)
    if lo > 0:
        m = jnp.maximum(m, jnp.max(s[:, :lo], axis=-1, keepdims=True))
    p = jnp.exp2(s_diag - m)
    if lo > 0:
        p = jnp.concatenate([jnp.exp2(s[:, :lo] - m), p], axis=1)
    r = jnp.dot(p.astype(BF16), v_ext[:hi], preferred_element_type=F32)
    o = r[:, :HEAD_DIM] / r[:, HEAD_DIM:]
    d = o[:tq] - lam * o[tq:]
    d = d * lax.rsqrt(jnp.mean(d * d, axis=-1, keepdims=True) + SUBLN_EPS) * g
    return d * (1.0 - lambda_init)


def _dilated_scores(q, k_ref, hs, hi, bias):
    return lax.dot_general(q, k_ref[0, :hi, hs], (((1,), (1,)), ((), ())),
                           preferred_element_type=F32) + bias


def _dilated_finish(s, v_ext, hi):
    p = jnp.exp2(s - jnp.max(s, axis=-1, keepdims=True))
    r = jnp.dot(p.astype(BF16), v_ext[:hi], preferred_element_type=F32)
    return r[:, :HEAD_DIM] / r[:, HEAD_DIM:]


def _split_ride_along(rest):
    n_ride = (len(rest) - 1) // 2
    _cast_ride_along(rest[:n_ride], rest[n_ride + 1:])
    return rest[n_ride]


def _diff_attn_kernel(lam_ref, g_ref, q_ref, k_ref, v_ref, *rest, tq, heads, lambda_init):
    o_ref = _split_ride_along(rest)
    seq = q_ref.shape[1]
    lv = lam_ref[...]
    lam = (jnp.exp(jnp.sum(lv[0:1] * lv[1:2], axis=-1, keepdims=True))
           - jnp.exp(jnp.sum(lv[2:3] * lv[3:4], axis=-1, keepdims=True)) + lambda_init)
    g = g_ref[...]
    row = lax.broadcasted_iota(jnp.int32, (2 * tq, tq), 0)
    col = lax.broadcasted_iota(jnp.int32, (2 * tq, tq), 1)
    visible = col <= jnp.where(row >= tq, row - tq, row)
    v_ext = [_with_ones_columns(v_ref[0, :, h * HEAD_DIM:(h + 1) * HEAD_DIM]) for h in range(heads)]

    def finish(h, lo, hi, s):
        o = _diff_finish(s, v_ext[h], lo, hi, visible, lam, g, lambda_init)
        o_ref[0, lo:hi, h * HEAD_DIM:(h + 1) * HEAD_DIM] = o.astype(o_ref.dtype)

    pending = None
    for h in range(heads):
        hs = slice(h * HEAD_DIM, (h + 1) * HEAD_DIM)
        for qb in reversed(range(seq // tq)):
            lo, hi = qb * tq, (qb + 1) * tq
            s = _diff_scores(q_ref[0, lo:hi, hs], k_ref, hs, hi)
            if pending is not None:
                finish(*pending)
            pending = (h, lo, hi, s)
    finish(*pending)


def _dil_attn_kernel(bias_ref, q_ref, k_ref, v_ref, *rest, tq, heads):
    o_ref = _split_ride_along(rest)
    seq = q_ref.shape[1]
    v_ext = [_with_ones_columns(v_ref[0, :, h * HEAD_DIM:(h + 1) * HEAD_DIM]) for h in range(heads)]

    def finish(h, lo, hi, s):
        o = _dilated_finish(s, v_ext[h], hi)
        o_ref[0, lo:hi, h * HEAD_DIM:(h + 1) * HEAD_DIM] = o.astype(o_ref.dtype)

    pending = None
    for h in range(heads):
        hs = slice(h * HEAD_DIM, (h + 1) * HEAD_DIM)
        for qb in reversed(range(seq // tq)):
            lo, hi = qb * tq, (qb + 1) * tq
            s = _dilated_scores(q_ref[0, lo:hi, hs], k_ref, hs, hi, bias_ref[:, seq - hi:])
            if pending is not None:
                finish(*pending)
            pending = (h, lo, hi, s)
    finish(*pending)


def _head_group_call(kern, name, small_inputs, proj, *, first_group, n_heads, heads, ride_along):
    b, s, _ = proj.shape
    assert n_heads % heads == 0
    w = heads * HEAD_DIM
    steps_per_row = n_heads // heads
    ride_in, ride_out, ride_shapes = _ride_along_specs(
        ride_along, b * steps_per_row, lambda bi, h: bi * steps_per_row + h)

    def group(k):
        return pl.BlockSpec((1, s, w), lambda bi, h: (bi, 0, k * steps_per_row + h))

    small_specs = [pl.BlockSpec(a.shape, lambda bi, h, nd=a.ndim: (0,) * nd) for a in small_inputs]
    return pl.pallas_call(
        kern,
        grid=(b, steps_per_row),
        in_specs=small_specs + [group(first_group + k) for k in range(3)] + ride_in,
        out_specs=[pl.BlockSpec((1, s, w), lambda bi, h: (bi, 0, h))] + ride_out,
        out_shape=[jax.ShapeDtypeStruct((b, s, n_heads * HEAD_DIM), BF16)] + ride_shapes,
        compiler_params=pltpu.CompilerParams(
            dimension_semantics=("parallel", "parallel"), vmem_limit_bytes=VMEM_LIMIT),
        name=name,
    )(*small_inputs, proj, proj, proj, *ride_along)


def _diff_attention(proj, lam_vecs, subln_g, *, n_heads, lambda_init, ride_along=(), tq=256, heads=2):
    assert proj.shape[1] % tq == 0
    kern = functools.partial(_diff_attn_kernel, tq=tq, heads=heads, lambda_init=lambda_init)
    return _head_group_call(kern, "diff_attn", [lam_vecs, subln_g], proj, first_group=0,
                            n_heads=n_heads, heads=heads, ride_along=ride_along)


def _dilated_attention(proj, *, n_heads, ride_along=(), tq=256, heads=2):
    s = proj.shape[1]
    assert s % tq == 0 and all(s % d == 0 for _, d in DILATION_CONFIGS)
    bias = jnp.asarray(_dilated_bias_table(s, tq))
    kern = functools.partial(_dil_attn_kernel, tq=tq, heads=heads)
    return _head_group_call(kern, "dilated_attn", [bias], proj, first_group=3,
                            n_heads=n_heads, heads=heads, ride_along=ride_along)


def _dilated_bias_table(seq, tq):
    dist = np.arange(tq)[:, None] + (seq - tq) - np.arange(seq)[None, :]
    count = np.zeros(dist.shape, np.int64)
    for window, dilation in DILATION_CONFIGS:
        count += (dist >= 0) & (dist <= window) & (dist % dilation == 0)
    return np.where(count > 0, np.log2(np.maximum(count, 1)), NEG).astype(np.float32)


def _layer_norm_rows(y, g, b):
    mu = jnp.mean(y, axis=-1, keepdims=True)
    yc = y - mu
    var = jnp.mean(yc * yc, axis=-1, keepdims=True)
    return yc * lax.rsqrt(var + LN_EPS) * g + b


def _out_ln_kernel(oa_ref, ob_ref, wa_ref, wb_ref, x_ref, g_ref, b_ref, y_ref, yb_ref,
                   *, alpha, sub_rows):
    for r0 in range(0, x_ref.shape[0], sub_rows):
        rows = slice(r0, r0 + sub_rows)
        attn = jnp.dot(oa_ref[rows, :], wa_ref[...], preferred_element_type=F32)
        attn = attn + jnp.dot(ob_ref[rows, :], wb_ref[...], preferred_element_type=F32)
        y = _layer_norm_rows(alpha * x_ref[rows, :] + attn, g_ref[...], b_ref[...])
        y_ref[rows, :] = y
        yb_ref[rows, :] = y.astype(yb_ref.dtype)


def _out_proj_ln(o_a, o_b, w_out_bf16, x2d, g, b, *, alpha, tm=512, sub_rows=256):
    m, d = x2d.shape
    ka, kb = o_a.shape[1], o_b.shape[1]
    assert m % tm == 0 and tm % sub_rows == 0 and w_out_bf16.shape == (ka + kb, d) and ka == kb
    const = dict(pipeline_mode=pl.Buffered(1))
    return pl.pallas_call(
        functools.partial(_out_ln_kernel, alpha=alpha, sub_rows=sub_rows),
        grid=(m // tm,),
        in_specs=[pl.BlockSpec((tm, ka), lambda i: (i, 0)),
                  pl.BlockSpec((tm, kb), lambda i: (i, 0)),
                  pl.BlockSpec((ka, d), lambda i: (0, 0), **const),
                  pl.BlockSpec((kb, d), lambda i: (1, 0), **const),
                  pl.BlockSpec((tm, d), lambda i: (i, 0)),
                  pl.BlockSpec((1, d), lambda i: (0, 0)),
                  pl.BlockSpec((1, d), lambda i: (0, 0))],
        out_specs=[pl.BlockSpec((tm, d), lambda i: (i, 0)), pl.BlockSpec((tm, d), lambda i: (i, 0))],
        out_shape=[jax.ShapeDtypeStruct((m, d), F32), jax.ShapeDtypeStruct((m, d), BF16)],
        compiler_params=pltpu.CompilerParams(
            dimension_semantics=("parallel",), vmem_limit_bytes=VMEM_LIMIT),
        name="out_proj_ln",
    )(o_a, o_b, w_out_bf16, w_out_bf16, x2d, g, b)


def _ffn_kernel(x_ref, xb_ref, w1_ref, w2_ref, g_ref, b_ref, o_ref, *, alpha, sub_rows):
    f = pl.program_id(1)
    last = pl.num_programs(1) - 1

    def mlp_chunk(rows):
        h = jnp.dot(xb_ref[rows, :], w1_ref[...], preferred_element_type=F32)
        h = jnp.square(jnp.maximum(h, 0.0)).astype(BF16)
        return jnp.dot(h, w2_ref[...], preferred_element_type=F32)

    everything = slice(None)

    @pl.when(f == 0)
    def _():
        o_ref[...] = alpha * x_ref[...] + mlp_chunk(everything)

    @pl.when((f > 0) & (f < last))
    def _():
        o_ref[...] += mlp_chunk(everything)

    @pl.when(f == last)
    def _():
        for r0 in range(0, x_ref.shape[0], sub_rows):
            rows = slice(r0, r0 + sub_rows)
            y = o_ref[rows, :] + mlp_chunk(rows)
            o_ref[rows, :] = _layer_norm_rows(y, g_ref[...], b_ref[...])


def _ffn_ln(x2d, xb2d, w1_bf16, w2_bf16, g, b, *, alpha, tm=512, tf=2048, sub_rows=256):
    m, d = x2d.shape
    dff = w1_bf16.shape[1]
    assert m % tm == 0 and tm % sub_rows == 0 and dff % tf == 0 and dff // tf >= 2
    assert xb2d.shape == x2d.shape
    return pl.pallas_call(
        functools.partial(_ffn_kernel, alpha=alpha, sub_rows=sub_rows),
        grid=(m // tm, dff // tf),
        in_specs=[pl.BlockSpec((tm, d), lambda i, f: (i, 0)),
                  pl.BlockSpec((tm, d), lambda i, f: (i, 0)),
                  pl.BlockSpec((d, tf), lambda i, f: (0, f)),
                  pl.BlockSpec((tf, d), lambda i, f: (f, 0)),
                  pl.BlockSpec((1, d), lambda i, f: (0, 0)),
                  pl.BlockSpec((1, d), lambda i, f: (0, 0))],
        out_specs=pl.BlockSpec((tm, d), lambda i, f: (i, 0)),
        out_shape=jax.ShapeDtypeStruct((m, d), F32),
        compiler_params=pltpu.CompilerParams(
            dimension_semantics=("parallel", "arbitrary"), vmem_limit_bytes=FFN_VMEM_LIMIT),
        name="ffn_ln",
    )(x2d, xb2d, w1_bf16, w2_bf16, g, b)


def kernel(x, w_in, lambda_q1, lambda_k1, lambda_q2, lambda_k2, subln_g, w_out, ln1_g, ln1_b,
           w_ff1, w_ff2, ln2_g, ln2_b):
    bsz, seq, d_model = x.shape
    depth = w_in.shape[0]
    n_heads = d_model // (2 * HEAD_DIM)
    width = n_heads * HEAD_DIM
    alpha = (2.0 * depth) ** 0.25

    h2d = x.reshape(bsz * seq, d_model)
    for l in range(depth):
        lambda_init = 0.8 - 0.6 * math.exp(-0.3 * l)
        proj = _projection(h2d, w_in[l].astype(BF16), seq).reshape(bsz, seq, 6 * width)
        lam_vecs = jnp.stack([lambda_q1[l], lambda_k1[l], lambda_q2[l], lambda_k2[l]]).astype(F32)
        o_a, w1_b = _diff_attention(
            proj, lam_vecs, subln_g[l].astype(F32).reshape(1, HEAD_DIM), n_heads=n_heads,
            lambda_init=lambda_init, ride_along=(w_ff1[l],))
        o_b, w2_b, wo_b = _dilated_attention(proj, n_heads=n_heads, ride_along=(w_ff2[l], w_out[l]))
        h2d, hb2d = _out_proj_ln(o_a.reshape(bsz * seq, width), o_b.reshape(bsz * seq, width),
                                 wo_b, h2d,
                                 ln1_g[l].reshape(1, d_model), ln1_b[l].reshape(1, d_model), alpha=alpha)
        h2d = _ffn_ln(h2d, hb2d, w1_b, w2_b,
                      ln2_g[l].reshape(1, d_model), ln2_b[l].reshape(1, d_model), alpha=alpha)
    return h2d.reshape(bsz, seq, d_model)
```

```python
import functools
import math

import numpy as np
import jax
import jax.numpy as jnp
from jax import lax
from jax.experimental import pallas as pl
from jax.experimental.pallas import tpu as pltpu

F32 = jnp.float32
BF16 = jnp.bfloat16

HEAD_DIM = 128
DIFF_QK_DIM = 64
ROPE_THETA = 10000.0
LN_EPS = 1e-5
SUBLN_EPS = 1e-5
NEG = -1e30
LOG2E = 1.4426950408889634
DILATION_CONFIGS = ((128, 1), (512, 4), (2048, 16))

VMEM_LIMIT = 56 * 1024 * 1024
FFN_VMEM_LIMIT = 62 * 1024 * 1024


def _proj_kernel(x_ref, w_ref, cosa_ref, sina_ref, cosb_ref, sinb_ref, o_ref,
                 *, group_cols, qa_scale, qb_scale):
    xb = x_ref[...].astype(BF16)
    cos_a, sin_a = cosa_ref[...], sina_ref[...]
    cos_b, sin_b = cosb_ref[...], sinb_ref[...]
    lane = lax.broadcasted_iota(jnp.int32, cos_a.shape, 1)
    first_half = (lane % DIFF_QK_DIM) < (DIFF_QK_DIM // 2)

    def rope_a(t):
        rot = jnp.where(first_half, pltpu.roll(t, 96, 1), pltpu.roll(t, 32, 1))
        return t * cos_a + rot * sin_a

    def rope_b(t):
        return t * cos_b + pltpu.roll(t, 64, 1) * sin_b

    groups = ((rope_a, qa_scale), (rope_a, None), (None, None),
              (rope_b, qb_scale), (rope_b, None), (None, None))
    for gi, (rope, scale) in enumerate(groups):
        c0 = gi * group_cols
        acc = jnp.dot(xb, w_ref[:, c0:c0 + group_cols], preferred_element_type=F32)
        for h in range(group_cols // HEAD_DIM):
            y = acc[:, h * HEAD_DIM:(h + 1) * HEAD_DIM]
            if rope is not None:
                y = rope(y)
            if scale is not None:
                y = y * scale
            o_ref[:, c0 + h * HEAD_DIM:c0 + (h + 1) * HEAD_DIM] = y.astype(o_ref.dtype)


def _rope_tables(seq):
    def ang(dim):
        inv = 1.0 / (ROPE_THETA ** (jnp.arange(0, dim, 2, dtype=F32) / dim))
        return jnp.arange(seq, dtype=F32)[:, None] * inv[None, :]
    a = ang(DIFF_QK_DIM)
    ca, sa = jnp.cos(a), jnp.sin(a)
    cos_a = jnp.concatenate([ca, ca, ca, ca], axis=-1)
    sin_a = jnp.concatenate([-sa, sa, -sa, sa], axis=-1)
    b = ang(HEAD_DIM)
    cb, sb = jnp.cos(b), jnp.sin(b)
    cos_b = jnp.concatenate([cb, cb], axis=-1)
    sin_b = jnp.concatenate([-sb, sb], axis=-1)
    return cos_a, sin_a, cos_b, sin_b


def _projection(x2d, w_bf16, seq, *, tm=512):
    m, k = x2d.shape
    n = w_bf16.shape[1]
    assert m % tm == 0 and seq % tm == 0 and n % (6 * HEAD_DIM) == 0
    cos_a, sin_a, cos_b, sin_b = _rope_tables(seq)
    pos_blocks = seq // tm
    tab_spec = pl.BlockSpec((tm, HEAD_DIM), lambda i: (i % pos_blocks, 0))
    kern = functools.partial(
        _proj_kernel, group_cols=n // 6,
        qa_scale=DIFF_QK_DIM ** -0.5 * LOG2E, qb_scale=HEAD_DIM ** -0.5 * LOG2E)
    return pl.pallas_call(
        kern,
        grid=(m // tm,),
        in_specs=[pl.BlockSpec((tm, k), lambda i: (i, 0)),
                  pl.BlockSpec((k, n), lambda i: (0, 0), pipeline_mode=pl.Buffered(1)),
                  tab_spec, tab_spec, tab_spec, tab_spec],
        out_specs=pl.BlockSpec((tm, n), lambda i: (i, 0)),
        out_shape=jax.ShapeDtypeStruct((m, n), BF16),
        compiler_params=pltpu.CompilerParams(
            dimension_semantics=("parallel",), vmem_limit_bytes=VMEM_LIMIT),
        name="proj_rope",
    )(x2d, w_bf16, cos_a, sin_a, cos_b, sin_b)


def _with_ones_columns(v):
    return jnp.concatenate([v, jnp.ones_like(v)], axis=1)


def _ride_along_specs(weights, n_steps, step_index):
    in_specs, out_specs, out_shapes = [], [], []
    for w in weights:
        rows, cols = w.shape
        assert rows % n_steps == 0 and (rows // n_steps) % 16 == 0
        spec = pl.BlockSpec((rows // n_steps, cols), lambda *idx: (step_index(*idx), 0))
        in_specs.append(spec)
        out_specs.append(spec)
        out_shapes.append(jax.ShapeDtypeStruct(w.shape, BF16))
    return in_specs, out_specs, out_shapes


def _cast_ride_along(src_refs, dst_refs):
    for src, dst in zip(src_refs, dst_refs):
        dst[...] = src[...].astype(dst.dtype)


def _diff_scores(q, k_ref, hs, hi):
    lane = lax.broadcasted_iota(jnp.int32, q.shape, 1)
    zero = jnp.zeros_like(q)
    q2 = jnp.concatenate([jnp.where(lane < DIFF_QK_DIM, q, zero),
                          jnp.where(lane >= DIFF_QK_DIM, q, zero)], axis=0)
    return lax.dot_general(q2, k_ref[0, :hi, hs], (((1,), (1,)), ((), ())),
                           preferred_element_type=F32)


def _diff_finish(s, v_ext, lo, hi, visible, lam, g, lambda_init):
    tq = hi - lo
    s_diag = jnp.where(visible, s[:, lo:], NEG)
    m = jnp.max(s_diag, axis=-1, keepdims=True)
    if lo > 0:
        m = jnp.maximum(m, jnp.max(s[:, :lo], axis=-1, keepdims=True))
    p = jnp.exp2(s_diag - m)
    if lo > 0:
        p = jnp.concatenate([jnp.exp2(s[:, :lo] - m), p], axis=1)
    r = jnp.dot(p.astype(BF16), v_ext[:hi], preferred_element_type=F32)
    o = r[:, :HEAD_DIM] / r[:, HEAD_DIM:]
    d = o[:tq] - lam * o[tq:]
    d = d * lax.rsqrt(jnp.mean(d * d, axis=-1, keepdims=True) + SUBLN_EPS) * g
    return d * (1.0 - lambda_init)


def _dilated_scores(q, k_ref, hs, hi, bias):
    return lax.dot_general(q, k_ref[0, :hi, hs], (((1,), (1,)), ((), ())),
                           preferred_element_type=F32) + bias


def _dilated_finish(s, v_ext, hi):
    p = jnp.exp2(s - jnp.max(s, axis=-1, keepdims=True))
    r = jnp.dot(p.astype(BF16), v_ext[:hi], preferred_element_type=F32)
    return r[:, :HEAD_DIM] / r[:, HEAD_DIM:]


def _split_ride_along(rest):
    n_ride = (len(rest) - 1) // 2
    _cast_ride_along(rest[:n_ride], rest[n_ride + 1:])
    return rest[n_ride]


def _diff_attn_kernel(lam_ref, g_ref, q_ref, k_ref, v_ref, *rest, tq, heads, lambda_init):
    o_ref = _split_ride_along(rest)
    seq = q_ref.shape[1]
    lv = lam_ref[...]
    lam = (jnp.exp(jnp.sum(lv[0:1] * lv[1:2], axis=-1, keepdims=True))
           - jnp.exp(jnp.sum(lv[2:3] * lv[3:4], axis=-1, keepdims=True)) + lambda_init)
    g = g_ref[...]
    row = lax.broadcasted_iota(jnp.int32, (2 * tq, tq), 0)
    col = lax.broadcasted_iota(jnp.int32, (2 * tq, tq), 1)
    visible = col <= jnp.where(row >= tq, row - tq, row)
    v_ext = [_with_ones_columns(v_ref[0, :, h * HEAD_DIM:(h + 1) * HEAD_DIM]) for h in range(heads)]

    def finish(h, lo, hi, s):
        o = _diff_finish(s, v_ext[h], lo, hi, visible, lam, g, lambda_init)
        o_ref[0, lo:hi, h * HEAD_DIM:(h + 1) * HEAD_DIM] = o.astype(o_ref.dtype)

    pending = None
    for h in range(heads):
        hs = slice(h * HEAD_DIM, (h + 1) * HEAD_DIM)
        for qb in reversed(range(seq // tq)):
            lo, hi = qb * tq, (qb + 1) * tq
            s = _diff_scores(q_ref[0, lo:hi, hs], k_ref, hs, hi)
            if pending is not None:
                finish(*pending)
            pending = (h, lo, hi, s)
    finish(*pending)


def _dil_attn_kernel(bias_ref, q_ref, k_ref, v_ref, *rest, tq, heads):
    o_ref = _split_ride_along(rest)
    seq = q_ref.shape[1]
    v_ext = [_with_ones_columns(v_ref[0, :, h * HEAD_DIM:(h + 1) * HEAD_DIM]) for h in range(heads)]

    def finish(h, lo, hi, s):
        o = _dilated_finish(s, v_ext[h], hi)
        o_ref[0, lo:hi, h * HEAD_DIM:(h + 1) * HEAD_DIM] = o.astype(o_ref.dtype)

    pending = None
    for h in range(heads):
        hs = slice(h * HEAD_DIM, (h + 1) * HEAD_DIM)
        for qb in reversed(range(seq // tq)):
            lo, hi = qb * tq, (qb + 1) * tq
            s = _dilated_scores(q_ref[0, lo:hi, hs], k_ref, hs, hi, bias_ref[:, seq - hi:])
            if pending is not None:
                finish(*pending)
            pending = (h, lo, hi, s)
    finish(*pending)


def _head_group_call(kern, name, small_inputs, proj, *, first_group, n_heads, heads, ride_along):
    b, s, _ = proj.shape
    assert n_heads % heads == 0
    w = heads * HEAD_DIM
    steps_per_row = n_heads // heads
    ride_in, ride_out, ride_shapes = _ride_along_specs(
        ride_along, b * steps_per_row, lambda bi, h: bi * steps_per_row + h)

    def group(k):
        return pl.BlockSpec((1, s, w), lambda bi, h: (bi, 0, k * steps_per_row + h))

    small_specs = [pl.BlockSpec(a.shape, lambda bi, h, nd=a.ndim: (0,) * nd) for a in small_inputs]
    return pl.pallas_call(
        kern,
        grid=(b, steps_per_row),
        in_specs=small_specs + [group(first_group + k) for k in range(3)] + ride_in,
        out_specs=[pl.BlockSpec((1, s, w), lambda bi, h: (bi, 0, h))] + ride_out,
        out_shape=[jax.ShapeDtypeStruct((b, s, n_heads * HEAD_DIM), BF16)] + ride_shapes,
        compiler_params=pltpu.CompilerParams(
            dimension_semantics=("parallel", "parallel"), vmem_limit_bytes=VMEM_LIMIT),
        name=name,
    )(*small_inputs, proj, proj, proj, *ride_along)


def _diff_attention(proj, lam_vecs, subln_g, *, n_heads, lambda_init, ride_along=(), tq=256, heads=2):
    assert proj.shape[1] % tq == 0
    kern = functools.partial(_diff_attn_kernel, tq=tq, heads=heads, lambda_init=lambda_init)
    return _head_group_call(kern, "diff_attn", [lam_vecs, subln_g], proj, first_group=0,
                            n_heads=n_heads, heads=heads, ride_along=ride_along)


def _dilated_attention(proj, *, n_heads, ride_along=(), tq=256, heads=2):
    s = proj.shape[1]
    assert s % tq == 0 and all(s % d == 0 for _, d in DILATION_CONFIGS)
    bias = jnp.asarray(_dilated_bias_table(s, tq))
    kern = functools.partial(_dil_attn_kernel, tq=tq, heads=heads)
    return _head_group_call(kern, "dilated_attn", [bias], proj, first_group=3,
                            n_heads=n_heads, heads=heads, ride_along=ride_along)


def _dilated_bias_table(seq, tq):
    dist = np.arange(tq)[:, None] + (seq - tq) - np.arange(seq)[None, :]
    count = np.zeros(dist.shape, np.int64)
    for window, dilation in DILATION_CONFIGS:
        count += (dist >= 0) & (dist <= window) & (dist % dilation == 0)
    return np.where(count > 0, np.log2(np.maximum(count, 1)), NEG).astype(np.float32)


def _layer_norm_rows(y, g, b):
    mu = jnp.mean(y, axis=-1, keepdims=True)
    yc = y - mu
    var = jnp.mean(yc * yc, axis=-1, keepdims=True)
    return yc * lax.rsqrt(var + LN_EPS) * g + b


def _out_ln_kernel(oa_ref, ob_ref, wa_ref, wb_ref, x_ref, g_ref, b_ref, y_ref, yb_ref,
                   *, alpha, sub_rows):
    for r0 in range(0, x_ref.shape[0], sub_rows):
        rows = slice(r0, r0 + sub_rows)
        attn = jnp.dot(oa_ref[rows, :], wa_ref[...], preferred_element_type=F32)
        attn = attn + jnp.dot(ob_ref[rows, :], wb_ref[...], preferred_element_type=F32)
        y = _layer_norm_rows(alpha * x_ref[rows, :] + attn, g_ref[...], b_ref[...])
        y_ref[rows, :] = y
        yb_ref[rows, :] = y.astype(yb_ref.dtype)


def _out_proj_ln(o_a, o_b, w_out_bf16, x2d, g, b, *, alpha, tm=512, sub_rows=256):
    m, d = x2d.shape
    ka, kb = o_a.shape[1], o_b.shape[1]
    assert m % tm == 0 and tm % sub_rows == 0 and w_out_bf16.shape == (ka + kb, d) and ka == kb
    const = dict(pipeline_mode=pl.Buffered(1))
    return pl.pallas_call(
        functools.partial(_out_ln_kernel, alpha=alpha, sub_rows=sub_rows),
        grid=(m // tm,),
        in_specs=[pl.BlockSpec((tm, ka), lambda i: (i, 0)),
                  pl.BlockSpec((tm, kb), lambda i: (i, 0)),
                  pl.BlockSpec((ka, d), lambda i: (0, 0), **const),
                  pl.BlockSpec((kb, d), lambda i: (1, 0), **const),
                  pl.BlockSpec((tm, d), lambda i: (i, 0)),
                  pl.BlockSpec((1, d), lambda i: (0, 0)),
                  pl.BlockSpec((1, d), lambda i: (0, 0))],
        out_specs=[pl.BlockSpec((tm, d), lambda i: (i, 0)), pl.BlockSpec((tm, d), lambda i: (i, 0))],
        out_shape=[jax.ShapeDtypeStruct((m, d), F32), jax.ShapeDtypeStruct((m, d), BF16)],
        compiler_params=pltpu.CompilerParams(
            dimension_semantics=("parallel",), vmem_limit_bytes=VMEM_LIMIT),
        name="out_proj_ln",
    )(o_a, o_b, w_out_bf16, w_out_bf16, x2d, g, b)


def _ffn_kernel(x_ref, xb_ref, w1_hbm, w2_hbm, g_ref, b_ref, o_ref, w1_buf, w2_buf, sem,
                *, alpha, sub_rows, tf, n_chunks):
    i = pl.program_id(0)

    def chunk_copies(f):
        slot = f % 2
        cols = pl.ds(f * tf, tf)
        return (pltpu.make_async_copy(w1_hbm.at[:, cols], w1_buf.at[slot], sem.at[0, slot]),
                pltpu.make_async_copy(w2_hbm.at[cols, :], w2_buf.at[slot], sem.at[1, slot]))

    def start(f):
        for c in chunk_copies(f):
            c.start()

    def wait(f):
        for c in chunk_copies(f):
            c.wait()

    @pl.when(i == 0)
    def _():
        start(0)
        start(1)

    for f in range(n_chunks):
        slot = f % 2
        wait(f)

        def mlp_chunk(rows, slot=slot):
            h = jnp.dot(xb_ref[rows, :], w1_buf[slot], preferred_element_type=F32)
            h = jnp.square(jnp.maximum(h, 0.0)).astype(BF16)
            return jnp.dot(h, w2_buf[slot], preferred_element_type=F32)

        if f == 0:
            o_ref[...] = alpha * x_ref[...] + mlp_chunk(slice(None))
        elif f < n_chunks - 1:
            o_ref[...] += mlp_chunk(slice(None))
        else:
            for r0 in range(0, x_ref.shape[0], sub_rows):
                rows = slice(r0, r0 + sub_rows)
                y = o_ref[rows, :] + mlp_chunk(rows)
                o_ref[rows, :] = _layer_norm_rows(y, g_ref[...], b_ref[...])
        start((f + 2) % n_chunks)

    @pl.when(i == pl.num_programs(0) - 1)
    def _():
        wait(0)
        wait(1)


def _ffn_ln(x2d, xb2d, w1_bf16, w2_bf16, g, b, *, alpha, tm=512, tf=2048, sub_rows=256):
    m, d = x2d.shape
    dff = w1_bf16.shape[1]
    n_chunks = dff // tf
    assert m % tm == 0 and tm % sub_rows == 0 and dff % tf == 0 and n_chunks >= 2 and n_chunks % 2 == 0
    assert xb2d.shape == x2d.shape
    return pl.pallas_call(
        functools.partial(_ffn_kernel, alpha=alpha, sub_rows=sub_rows, tf=tf, n_chunks=n_chunks),
        grid=(m // tm,),
        in_specs=[pl.BlockSpec((tm, d), lambda i: (i, 0)),
                  pl.BlockSpec((tm, d), lambda i: (i, 0)),
                  pl.BlockSpec(memory_space=pl.ANY),
                  pl.BlockSpec(memory_space=pl.ANY),
                  pl.BlockSpec((1, d), lambda i: (0, 0)),
                  pl.BlockSpec((1, d), lambda i: (0, 0))],
        out_specs=pl.BlockSpec((tm, d), lambda i: (i, 0)),
        out_shape=jax.ShapeDtypeStruct((m, d), F32),
        scratch_shapes=[pltpu.VMEM((2, d, tf), BF16), pltpu.VMEM((2, tf, d), BF16),
                        pltpu.SemaphoreType.DMA((2, 2))],
        compiler_params=pltpu.CompilerParams(
            dimension_semantics=("arbitrary",), vmem_limit_bytes=FFN_VMEM_LIMIT),
        name="ffn_ln",
    )(x2d, xb2d, w1_bf16, w2_bf16, g, b)


def kernel(x, w_in, lambda_q1, lambda_k1, lambda_q2, lambda_k2, subln_g, w_out, ln1_g, ln1_b,
           w_ff1, w_ff2, ln2_g, ln2_b):
    bsz, seq, d_model = x.shape
    depth = w_in.shape[0]
    n_heads = d_model // (2 * HEAD_DIM)
    width = n_heads * HEAD_DIM
    alpha = (2.0 * depth) ** 0.25

    h2d = x.reshape(bsz * seq, d_model)
    for l in range(depth):
        lambda_init = 0.8 - 0.6 * math.exp(-0.3 * l)
        proj = _projection(h2d, w_in[l].astype(BF16), seq).reshape(bsz, seq, 6 * width)
        lam_vecs = jnp.stack([lambda_q1[l], lambda_k1[l], lambda_q2[l], lambda_k2[l]]).astype(F32)
        o_a, w1_b = _diff_attention(
            proj, lam_vecs, subln_g[l].astype(F32).reshape(1, HEAD_DIM), n_heads=n_heads,
            lambda_init=lambda_init, ride_along=(w_ff1[l],))
        o_b, w2_b, wo_b = _dilated_attention(proj, n_heads=n_heads, ride_along=(w_ff2[l], w_out[l]))
        h2d, hb2d = _out_proj_ln(o_a.reshape(bsz * seq, width), o_b.reshape(bsz * seq, width),
                                 wo_b, h2d,
                                 ln1_g[l].reshape(1, d_model), ln1_b[l].reshape(1, d_model), alpha=alpha)
        h2d = _ffn_ln(h2d, hb2d, w1_b, w2_b,
                      ln2_g[l].reshape(1, d_model), ln2_b[l].reshape(1, d_model), alpha=alpha)
    return h2d.reshape(bsz, seq, d_model)
```

```python
import functools
import math

import numpy as np
import jax
import jax.numpy as jnp
from jax import lax
from jax.experimental import pallas as pl
from jax.experimental.pallas import tpu as pltpu

F32 = jnp.float32
BF16 = jnp.bfloat16

HEAD_DIM = 128
DIFF_QK_DIM = 64
ROPE_THETA = 10000.0
LN_EPS = 1e-5
SUBLN_EPS = 1e-5
NEG = -1e30
LOG2E = 1.4426950408889634
DILATION_CONFIGS = ((128, 1), (512, 4), (2048, 16))

VMEM_LIMIT = 56 * 1024 * 1024
FFN_VMEM_LIMIT = 62 * 1024 * 1024


def _proj_kernel(x_ref, w_ref, cosa_ref, sina_ref, cosb_ref, sinb_ref, o_ref,
                 *, group_cols, qa_scale, qb_scale):
    xb = x_ref[...].astype(BF16)
    cos_a, sin_a = cosa_ref[...], sina_ref[...]
    cos_b, sin_b = cosb_ref[...], sinb_ref[...]
    lane = lax.broadcasted_iota(jnp.int32, cos_a.shape, 1)
    first_half = (lane % DIFF_QK_DIM) < (DIFF_QK_DIM // 2)

    def rope_a(t):
        rot = jnp.where(first_half, pltpu.roll(t, 96, 1), pltpu.roll(t, 32, 1))
        return t * cos_a + rot * sin_a

    def rope_b(t):
        return t * cos_b + pltpu.roll(t, 64, 1) * sin_b

    groups = ((rope_a, qa_scale), (rope_a, None), (None, None),
              (rope_b, qb_scale), (rope_b, None), (None, None))
    for gi, (rope, scale) in enumerate(groups):
        c0 = gi * group_cols
        acc = jnp.dot(xb, w_ref[:, c0:c0 + group_cols], preferred_element_type=F32)
        for h in range(group_cols // HEAD_DIM):
            y = acc[:, h * HEAD_DIM:(h + 1) * HEAD_DIM]
            if rope is not None:
                y = rope(y)
            if scale is not None:
                y = y * scale
            o_ref[:, c0 + h * HEAD_DIM:c0 + (h + 1) * HEAD_DIM] = y.astype(o_ref.dtype)


def _rope_tables(seq):
    def ang(dim):
        inv = 1.0 / (ROPE_THETA ** (jnp.arange(0, dim, 2, dtype=F32) / dim))
        return jnp.arange(seq, dtype=F32)[:, None] * inv[None, :]
    a = ang(DIFF_QK_DIM)
    ca, sa = jnp.cos(a), jnp.sin(a)
    cos_a = jnp.concatenate([ca, ca, ca, ca], axis=-1)
    sin_a = jnp.concatenate([-sa, sa, -sa, sa], axis=-1)
    b = ang(HEAD_DIM)
    cb, sb = jnp.cos(b), jnp.sin(b)
    cos_b = jnp.concatenate([cb, cb], axis=-1)
    sin_b = jnp.concatenate([-sb, sb], axis=-1)
    return cos_a, sin_a, cos_b, sin_b


def _projection(x2d, w_bf16, seq, *, tm=512):
    m, k = x2d.shape
    n = w_bf16.shape[1]
    assert m % tm == 0 and seq % tm == 0 and n % (6 * HEAD_DIM) == 0
    cos_a, sin_a, cos_b, sin_b = _rope_tables(seq)
    pos_blocks = seq // tm
    tab_spec = pl.BlockSpec((tm, HEAD_DIM), lambda i: (i % pos_blocks, 0))
    kern = functools.partial(
        _proj_kernel, group_cols=n // 6,
        qa_scale=DIFF_QK_DIM ** -0.5 * LOG2E, qb_scale=HEAD_DIM ** -0.5 * LOG2E)
    return pl.pallas_call(
        kern,
        grid=(m // tm,),
        in_specs=[pl.BlockSpec((tm, k), lambda i: (i, 0)),
                  pl.BlockSpec((k, n), lambda i: (0, 0), pipeline_mode=pl.Buffered(1)),
                  tab_spec, tab_spec, tab_spec, tab_spec],
        out_specs=pl.BlockSpec((tm, n), lambda i: (i, 0)),
        out_shape=jax.ShapeDtypeStruct((m, n), BF16),
        compiler_params=pltpu.CompilerParams(
            dimension_semantics=("parallel",), vmem_limit_bytes=VMEM_LIMIT),
        name="proj_rope",
    )(x2d, w_bf16, cos_a, sin_a, cos_b, sin_b)


def _with_ones_columns(v):
    return jnp.concatenate([v, jnp.ones_like(v)], axis=1)


def _ride_along_specs(weights, n_steps, step_index):
    in_specs, out_specs, out_shapes = [], [], []
    for w in weights:
        rows, cols = w.shape
        assert rows % n_steps == 0 and (rows // n_steps) % 16 == 0
        spec = pl.BlockSpec((rows // n_steps, cols), lambda *idx: (step_index(*idx), 0))
        in_specs.append(spec)
        out_specs.append(spec)
        out_shapes.append(jax.ShapeDtypeStruct(w.shape, BF16))
    return in_specs, out_specs, out_shapes


def _cast_ride_along(src_refs, dst_refs):
    for src, dst in zip(src_refs, dst_refs):
        dst[...] = src[...].astype(dst.dtype)


def _diff_scores(q, k_ref, hs, hi):
    lane = lax.broadcasted_iota(jnp.int32, q.shape, 1)
    zero = jnp.zeros_like(q)
    q2 = jnp.concatenate([jnp.where(lane < DIFF_QK_DIM, q, zero),
                          jnp.where(lane >= DIFF_QK_DIM, q, zero)], axis=0)
    return lax.dot_general(q2, k_ref[0, :hi, hs], (((1,), (1,)), ((), ())),
                           preferred_element_type=F32)


def _diff_finish(s, v_ext, lo, hi, visible, lam, g, lambda_init):
    tq = hi - lo
    s_diag = jnp.where(visible, s[:, lo:], NEG)
    m = jnp.max(s_diag, axis=-1, keepdims=True)
    if lo > 0:
        m = jnp.maximum(m, jnp.max(s[:, :lo], axis=-1, keepdims=True))
    p = jnp.exp2(s_diag - m)
    if lo > 0:
        p = jnp.concatenate([jnp.exp2(s[:, :lo] - m), p], axis=1)
    r = jnp.dot(p.astype(BF16), v_ext[:hi], preferred_element_type=F32)
    o = r[:, :HEAD_DIM] / r[:, HEAD_DIM:]
    d = o[:tq] - lam * o[tq:]
    d = d * lax.rsqrt(jnp.mean(d * d, axis=-1, keepdims=True) + SUBLN_EPS) * g
    return d * (1.0 - lambda_init)


def _dilated_scores(q, k_ref, hs, hi, bias):
    return lax.dot_general(q, k_ref[0, :hi, hs], (((1,), (1,)), ((), ())),
                           preferred_element_type=F32) + bias


def _dilated_finish(s, v_ext, hi):
    p = jnp.exp2(s - jnp.max(s, axis=-1, keepdims=True))
    r = jnp.dot(p.astype(BF16), v_ext[:hi], preferred_element_type=F32)
    return r[:, :HEAD_DIM] / r[:, HEAD_DIM:]


def _split_ride_along(rest):
    n_ride = (len(rest) - 1) // 2
    _cast_ride_along(rest[:n_ride], rest[n_ride + 1:])
    return rest[n_ride]


def _diff_attn_kernel(lam_ref, g_ref, q_ref, k_ref, v_ref, *rest, tq, heads, lambda_init):
    o_ref = _split_ride_along(rest)
    seq = q_ref.shape[1]
    lv = lam_ref[...]
    lam = (jnp.exp(jnp.sum(lv[0:1] * lv[1:2], axis=-1, keepdims=True))
           - jnp.exp(jnp.sum(lv[2:3] * lv[3:4], axis=-1, keepdims=True)) + lambda_init)
    g = g_ref[...]
    row = lax.broadcasted_iota(jnp.int32, (2 * tq, tq), 0)
    col = lax.broadcasted_iota(jnp.int32, (2 * tq, tq), 1)
    visible = col <= jnp.where(row >= tq, row - tq, row)
    v_ext = [_with_ones_columns(v_ref[0, :, h * HEAD_DIM:(h + 1) * HEAD_DIM]) for h in range(heads)]

    def finish(h, lo, hi, s):
        o = _diff_finish(s, v_ext[h], lo, hi, visible, lam, g, lambda_init)
        o_ref[0, lo:hi, h * HEAD_DIM:(h + 1) * HEAD_DIM] = o.astype(o_ref.dtype)

    pending = None
    for h in range(heads):
        hs = slice(h * HEAD_DIM, (h + 1) * HEAD_DIM)
        for qb in reversed(range(seq // tq)):
            lo, hi = qb * tq, (qb + 1) * tq
            s = _diff_scores(q_ref[0, lo:hi, hs], k_ref, hs, hi)
            if pending is not None:
                finish(*pending)
            pending = (h, lo, hi, s)
    finish(*pending)


def _dil_attn_kernel(bias_ref, q_ref, k_ref, v_ref, *rest, tq, heads):
    o_ref = _split_ride_along(rest)
    seq = q_ref.shape[1]
    v_ext = [_with_ones_columns(v_ref[0, :, h * HEAD_DIM:(h + 1) * HEAD_DIM]) for h in range(heads)]

    def finish(h, lo, hi, s):
        o = _dilated_finish(s, v_ext[h], hi)
        o_ref[0, lo:hi, h * HEAD_DIM:(h + 1) * HEAD_DIM] = o.astype(o_ref.dtype)

    pending = None
    for h in range(heads):
        hs = slice(h * HEAD_DIM, (h + 1) * HEAD_DIM)
        for qb in reversed(range(seq // tq)):
            lo, hi = qb * tq, (qb + 1) * tq
            s = _dilated_scores(q_ref[0, lo:hi, hs], k_ref, hs, hi, bias_ref[:, seq - hi:])
            if pending is not None:
                finish(*pending)
            pending = (h, lo, hi, s)
    finish(*pending)


def _head_group_call(kern, name, small_inputs, proj, *, first_group, n_heads, heads, ride_along):
    b, s, _ = proj.shape
    assert n_heads % heads == 0
    w = heads * HEAD_DIM
    steps_per_row = n_heads // heads
    ride_in, ride_out, ride_shapes = _ride_along_specs(
        ride_along, b * steps_per_row, lambda bi, h: bi * steps_per_row + h)

    def group(k):
        return pl.BlockSpec((1, s, w), lambda bi, h: (bi, 0, k * steps_per_row + h))

    small_specs = [pl.BlockSpec(a.shape, lambda bi, h, nd=a.ndim: (0,) * nd) for a in small_inputs]
    return pl.pallas_call(
        kern,
        grid=(b, steps_per_row),
        in_specs=small_specs + [group(first_group + k) for k in range(3)] + ride_in,
        out_specs=[pl.BlockSpec((1, s, w), lambda bi, h: (bi, 0, h))] + ride_out,
        out_shape=[jax.ShapeDtypeStruct((b, s, n_heads * HEAD_DIM), BF16)] + ride_shapes,
        compiler_params=pltpu.CompilerParams(
            dimension_semantics=("parallel", "parallel"), vmem_limit_bytes=VMEM_LIMIT),
        name=name,
    )(*small_inputs, proj, proj, proj, *ride_along)


def _diff_attention(proj, lam_vecs, subln_g, *, n_heads, lambda_init, ride_along=(), tq=256, heads=2):
    assert proj.shape[1] % tq == 0
    kern = functools.partial(_diff_attn_kernel, tq=tq, heads=heads, lambda_init=lambda_init)
    return _head_group_call(kern, "diff_attn", [lam_vecs, subln_g], proj, first_group=0,
                            n_heads=n_heads, heads=heads, ride_along=ride_along)


def _dilated_attention(proj, *, n_heads, ride_along=(), tq=256, heads=2):
    s = proj.shape[1]
    assert s % tq == 0 and all(s % d == 0 for _, d in DILATION_CONFIGS)
    bias = jnp.asarray(_dilated_bias_table(s, tq))
    kern = functools.partial(_dil_attn_kernel, tq=tq, heads=heads)
    return _head_group_call(kern, "dilated_attn", [bias], proj, first_group=3,
                            n_heads=n_heads, heads=heads, ride_along=ride_along)


def _dilated_bias_table(seq, tq):
    dist = np.arange(tq)[:, None] + (seq - tq) - np.arange(seq)[None, :]
    count = np.zeros(dist.shape, np.int64)
    for window, dilation in DILATION_CONFIGS:
        count += (dist >= 0) & (dist <= window) & (dist % dilation == 0)
    return np.where(count > 0, np.log2(np.maximum(count, 1)), NEG).astype(np.float32)


def _layer_norm_rows(y, g, b):
    mu = jnp.mean(y, axis=-1, keepdims=True)
    yc = y - mu
    var = jnp.mean(yc * yc, axis=-1, keepdims=True)
    return yc * lax.rsqrt(var + LN_EPS) * g + b


def _out_ln_kernel(oa_ref, ob_ref, wa_ref, wb_ref, x_ref, g_ref, b_ref, y_ref, yb_ref,
                   *, alpha, sub_rows):
    for r0 in range(0, x_ref.shape[0], sub_rows):
        rows = slice(r0, r0 + sub_rows)
        mix = jnp.concatenate([oa_ref[rows, :], ob_ref[rows, :]], axis=1)
        w = jnp.concatenate([wa_ref[...], wb_ref[...]], axis=0)
        attn = jnp.dot(mix, w, preferred_element_type=F32)
        y = _layer_norm_rows(alpha * x_ref[rows, :] + attn, g_ref[...], b_ref[...])
        y_ref[rows, :] = y
        yb_ref[rows, :] = y.astype(yb_ref.dtype)


def _out_proj_ln(o_a, o_b, w_out_bf16, x2d, g, b, *, alpha, tm=512, sub_rows=256):
    m, d = x2d.shape
    ka, kb = o_a.shape[1], o_b.shape[1]
    assert m % tm == 0 and tm % sub_rows == 0 and w_out_bf16.shape == (ka + kb, d) and ka == kb
    const = dict(pipeline_mode=pl.Buffered(1))
    return pl.pallas_call(
        functools.partial(_out_ln_kernel, alpha=alpha, sub_rows=sub_rows),
        grid=(m // tm,),
        in_specs=[pl.BlockSpec((tm, ka), lambda i: (i, 0)),
                  pl.BlockSpec((tm, kb), lambda i: (i, 0)),
                  pl.BlockSpec((ka, d), lambda i: (0, 0), **const),
                  pl.BlockSpec((kb, d), lambda i: (1, 0), **const),
                  pl.BlockSpec((tm, d), lambda i: (i, 0)),
                  pl.BlockSpec((1, d), lambda i: (0, 0)),
                  pl.BlockSpec((1, d), lambda i: (0, 0))],
        out_specs=[pl.BlockSpec((tm, d), lambda i: (i, 0)), pl.BlockSpec((tm, d), lambda i: (i, 0))],
        out_shape=[jax.ShapeDtypeStruct((m, d), F32), jax.ShapeDtypeStruct((m, d), BF16)],
        compiler_params=pltpu.CompilerParams(
            dimension_semantics=("parallel",), vmem_limit_bytes=VMEM_LIMIT),
        name="out_proj_ln",
    )(o_a, o_b, w_out_bf16, w_out_bf16, x2d, g, b)


def _ffn_kernel(x_ref, xb_ref, w1_hbm, w2_hbm, g_ref, b_ref, o_ref, w1_buf, w2_buf, sem,
                *, alpha, sub_rows, tf, n_chunks):
    i = pl.program_id(0)

    def chunk_copies(f):
        slot = f % 2
        cols = pl.ds(f * tf, tf)
        return (pltpu.make_async_copy(w1_hbm.at[:, cols], w1_buf.at[slot], sem.at[0, slot]),
                pltpu.make_async_copy(w2_hbm.at[cols, :], w2_buf.at[slot], sem.at[1, slot]))

    def start(f):
        for c in chunk_copies(f):
            c.start()

    def wait(f):
        for c in chunk_copies(f):
            c.wait()

    @pl.when(i == 0)
    def _():
        start(0)
        start(1)

    for f in range(n_chunks):
        slot = f % 2
        wait(f)

        def mlp_chunk(rows, slot=slot):
            h = jnp.dot(xb_ref[rows, :], w1_buf[slot], preferred_element_type=F32)
            h = jnp.square(jnp.maximum(h, 0.0)).astype(BF16)
            return jnp.dot(h, w2_buf[slot], preferred_element_type=F32)

        if f == 0:
            o_ref[...] = alpha * x_ref[...] + mlp_chunk(slice(None))
        elif f < n_chunks - 1:
            o_ref[...] += mlp_chunk(slice(None))
        else:
            for r0 in range(0, x_ref.shape[0], sub_rows):
                rows = slice(r0, r0 + sub_rows)
                y = o_ref[rows, :] + mlp_chunk(rows)
                o_ref[rows, :] = _layer_norm_rows(y, g_ref[...], b_ref[...])
        start((f + 2) % n_chunks)

    @pl.when(i == pl.num_programs(0) - 1)
    def _():
        wait(0)
        wait(1)


def _ffn_ln(x2d, xb2d, w1_bf16, w2_bf16, g, b, *, alpha, tm=512, tf=2048, sub_rows=256):
    m, d = x2d.shape
    dff = w1_bf16.shape[1]
    n_chunks = dff // tf
    assert m % tm == 0 and tm % sub_rows == 0 and dff % tf == 0 and n_chunks >= 2 and n_chunks % 2 == 0
    assert xb2d.shape == x2d.shape
    return pl.pallas_call(
        functools.partial(_ffn_kernel, alpha=alpha, sub_rows=sub_rows, tf=tf, n_chunks=n_chunks),
        grid=(m // tm,),
        in_specs=[pl.BlockSpec((tm, d), lambda i: (i, 0)),
                  pl.BlockSpec((tm, d), lambda i: (i, 0)),
                  pl.BlockSpec(memory_space=pl.ANY),
                  pl.BlockSpec(memory_space=pl.ANY),
                  pl.BlockSpec((1, d), lambda i: (0, 0)),
                  pl.BlockSpec((1, d), lambda i: (0, 0))],
        out_specs=pl.BlockSpec((tm, d), lambda i: (i, 0)),
        out_shape=jax.ShapeDtypeStruct((m, d), F32),
        scratch_shapes=[pltpu.VMEM((2, d, tf), BF16), pltpu.VMEM((2, tf, d), BF16),
                        pltpu.SemaphoreType.DMA((2, 2))],
        compiler_params=pltpu.CompilerParams(
            dimension_semantics=("arbitrary",), vmem_limit_bytes=FFN_VMEM_LIMIT),
        name="ffn_ln",
    )(x2d, xb2d, w1_bf16, w2_bf16, g, b)


def kernel(x, w_in, lambda_q1, lambda_k1, lambda_q2, lambda_k2, subln_g, w_out, ln1_g, ln1_b,
           w_ff1, w_ff2, ln2_g, ln2_b):
    bsz, seq, d_model = x.shape
    depth = w_in.shape[0]
    n_heads = d_model // (2 * HEAD_DIM)
    width = n_heads * HEAD_DIM
    alpha = (2.0 * depth) ** 0.25

    h2d = x.reshape(bsz * seq, d_model)
    for l in range(depth):
        lambda_init = 0.8 - 0.6 * math.exp(-0.3 * l)
        proj = _projection(h2d, w_in[l].astype(BF16), seq).reshape(bsz, seq, 6 * width)
        lam_vecs = jnp.stack([lambda_q1[l], lambda_k1[l], lambda_q2[l], lambda_k2[l]]).astype(F32)
        o_a, w1_b = _diff_attention(
            proj, lam_vecs, subln_g[l].astype(F32).reshape(1, HEAD_DIM), n_heads=n_heads,
            lambda_init=lambda_init, ride_along=(w_ff1[l],))
        o_b, w2_b, wo_b = _dilated_attention(proj, n_heads=n_heads, ride_along=(w_ff2[l], w_out[l]))
        h2d, hb2d = _out_proj_ln(o_a.reshape(bsz * seq, width), o_b.reshape(bsz * seq, width),
                                 wo_b, h2d,
                                 ln1_g[l].reshape(1, d_model), ln1_b[l].reshape(1, d_model), alpha=alpha)
        h2d = _ffn_ln(h2d, hb2d, w1_b, w2_b,
                      ln2_g[l].reshape(1, d_model), ln2_b[l].reshape(1, d_model), alpha=alpha)
    return h2d.reshape(bsz, seq, d_model)
```
